```python
import math
import jax, jax.numpy as jnp
from jax import lax
import numpy as np

D_MODEL = 1024
BATCH = 32
SEQ = 2048
DEPTH = 4

N_A_LAYERS = max(1, DEPTH // 2)
N_B_LAYERS = DEPTH - N_A_LAYERS
POOL_WINDOWS = (2, 4, 8, 16)
N_POOL_GROUPS = len(POOL_WINDOWS)
POOL_GROUP_DIM = D_MODEL // N_POOL_GROUPS
HEAD_DIM = 64
N_HEADS = D_MODEL // HEAD_DIM
DILATED_GROUPS = ((128, 1), (512, 4), (2048, 16))
N_GROUPS = len(DILATED_GROUPS)
ATTN_DIM = N_HEADS * HEAD_DIM
Q_DIM = N_GROUPS * ATTN_DIM
ROPE_THETA = 10000.0
D_FF = 2816
CONV_WIDTH = 3
DEEPNORM_ALPHA = (2.0 * DEPTH) ** 0.25
DEEPNORM_BETA = (8.0 * DEPTH) ** -0.25
LN_EPS = 1e-5

kernel_name = "yoco_pool_dilated_attn_convffn_deepnorm"


def layer_norm(x, g, b):
    xf = x.astype(jnp.float32)
    mu = xf.mean(-1, keepdims=True)
    var = jnp.square(xf - mu).mean(-1, keepdims=True)
    y = (xf - mu) * lax.rsqrt(var + LN_EPS) * g.astype(jnp.float32) + b.astype(jnp.float32)
    return y.astype(x.dtype)


def rope_tables(seq):
    inv_freq = ROPE_THETA ** (-jnp.arange(0, HEAD_DIM, 2, dtype=jnp.float32) / HEAD_DIM)
    ang = jnp.arange(seq, dtype=jnp.float32)[:, None] * inv_freq[None, :]
    return jnp.cos(ang), jnp.sin(ang)


def apply_rope(t, cos, sin):
    tf = t.astype(jnp.float32)
    x1, x2 = tf[..., : HEAD_DIM // 2], tf[..., HEAD_DIM // 2:]
    c, s = cos[None, :, None, :], sin[None, :, None, :]
    return jnp.concatenate([x1 * c - x2 * s, x2 * c + x1 * s], axis=-1).astype(t.dtype)


def pool_mixer(x, pool_w, pool_scale):
    B, S, D = x.shape
    xg = x.reshape(B, S, N_POOL_GROUPS, POOL_GROUP_DIM)
    csum = jnp.cumsum(xg.astype(jnp.float32), axis=1)
    c0 = jnp.concatenate([jnp.zeros_like(csum[:, :1]), csum], axis=1)
    pos = jnp.arange(S, dtype=jnp.float32)
    pooled = []
    for g, w in enumerate(POOL_WINDOWS):
        w_eff = min(w, S)
        lagged = jnp.concatenate([jnp.zeros_like(c0[:, : w_eff - 1, g]), c0[:, : S - w_eff + 1, g]], axis=1)
        count = jnp.minimum(pos + 1.0, float(w))[None, :, None]
        pooled.append((c0[:, 1:, g] - lagged) / count)
    pooled = jnp.stack(pooled, axis=2).astype(x.dtype) - xg
    y = jnp.einsum('bsgc,gce->bsge', pooled, pool_w).reshape(B, S, D)
    return y * pool_scale


def dilated_branch(q, k, v, dilation, span):
    B, S, H, Dh = q.shape
    L = S // dilation
    nb = -(-L // span)
    pad = nb * span - L

    def strided_blocks(t):
        t = t.reshape(B, L, dilation, H, Dh).transpose(0, 2, 3, 1, 4)
        t = jnp.pad(t, ((0, 0), (0, 0), (0, 0), (0, pad), (0, 0)))
        return t.reshape(B, dilation, H, nb, span, Dh)

    def with_prev(t):
        prev = jnp.concatenate([jnp.zeros_like(t[:, :, :, :1]), t[:, :, :, :-1]], axis=3)
        return jnp.concatenate([prev, t], axis=4)

    qb = strided_blocks(q)
    kk = with_prev(strided_blocks(k))
    vv = with_prev(strided_blocks(v))
    s = jnp.einsum('brhnqc,brhnkc->brhnqk', qb, kk, preferred_element_type=jnp.float32)
    qi = jnp.arange(span)[:, None]
    kj = jnp.arange(2 * span)[None, :]
    rel = span + qi - kj
    band = (rel >= 0) & (rel <= span)
    has_prev = (jnp.arange(nb) > 0)[:, None, None] | (kj >= span)[None]
    valid = band[None] & has_prev
    s = jnp.where(valid, s, -jnp.inf)
    m = s.max(-1, keepdims=True)
    p = jnp.exp(s - m)
    l = p.sum(-1, keepdims=True)
    o = jnp.einsum('brhnqk,brhnkc->brhnqc', (p / l).astype(v.dtype), vv)
    lse = (m + jnp.log(l))[..., 0]
    o = o.reshape(B, dilation, H, nb * span, Dh)[:, :, :, :L].transpose(0, 3, 1, 2, 4).reshape(B, S, H, Dh)
    lse = lse.reshape(B, dilation, H, nb * span)[..., :L].transpose(0, 3, 1, 2).reshape(B, S, H)
    return o, lse


def dilated_attention(x, k_shared, v_shared, w_q, w_o, cos, sin):
    B, S, _ = x.shape
    q = (x @ w_q).reshape(B, S, N_GROUPS * N_HEADS, HEAD_DIM)
    q = (apply_rope(q, cos, sin) * (HEAD_DIM ** -0.5)).reshape(B, S, N_GROUPS, N_HEADS, HEAD_DIM)
    outs, lses = [], []
    for g, (window, dilation) in enumerate(DILATED_GROUPS):
        o, lse = dilated_branch(q[:, :, g], k_shared[:, :, g], v_shared[:, :, g], dilation, window // dilation)
        outs.append(o)
        lses.append(lse)
    weights = jax.nn.softmax(jnp.stack(lses, axis=0), axis=0)
    o = jnp.sum(weights[..., None].astype(x.dtype) * jnp.stack(outs, axis=0), axis=0)
    return o.reshape(B, S, ATTN_DIM) @ w_o


def conv_ffn(x, w_gate, w_up, conv_w, conv_b, w_down):
    S = x.shape[1]
    g = x @ w_gate
    u = x @ w_up
    gp = jnp.pad(g, ((0, 0), (CONV_WIDTH - 1, 0), (0, 0)))
    conv = conv_b
    for j in range(CONV_WIDTH):
        conv = conv + conv_w[j] * gp[:, j: j + S]
    h = jax.nn.gelu(conv) * u
    return h @ w_down


def setup_inputs(seed: int = 0) -> dict:
    key = jax.random.key(seed)
    ks = jax.random.split(key, 16)
    f32 = jnp.float32
    beta = DEEPNORM_BETA
    nrm = lambda k, shape, scale: jax.random.normal(k, shape, f32) * scale
    x = jax.random.normal(ks[0], (BATCH, SEQ, D_MODEL), f32)
    pool_w = nrm(ks[1], (N_A_LAYERS, N_POOL_GROUPS, POOL_GROUP_DIM, POOL_GROUP_DIM), beta * POOL_GROUP_DIM ** -0.5)
    pool_scale = 1.0 + nrm(ks[2], (N_A_LAYERS, D_MODEL), 0.1)
    w_q = nrm(ks[3], (N_B_LAYERS, D_MODEL, Q_DIM), D_MODEL ** -0.5)
    w_k = nrm(ks[4], (D_MODEL, Q_DIM), D_MODEL ** -0.5)
    w_v = nrm(ks[5], (D_MODEL, Q_DIM), beta * D_MODEL ** -0.5)
    w_kv = jnp.concatenate([w_k, w_v], axis=1)
    w_o = nrm(ks[6], (N_B_LAYERS, ATTN_DIM, D_MODEL), beta * ATTN_DIM ** -0.5)
    ffn_w_gate = nrm(ks[7], (DEPTH, D_MODEL, D_FF), D_MODEL ** -0.5)
    ffn_w_up = nrm(ks[8], (DEPTH, D_MODEL, D_FF), beta * D_MODEL ** -0.5)
    ffn_conv_w = nrm(ks[9], (DEPTH, CONV_WIDTH, D_FF), CONV_WIDTH ** -0.5)
    ffn_conv_b = nrm(ks[10], (DEPTH, D_FF), 0.02)
    ffn_w_down = nrm(ks[11], (DEPTH, D_FF, D_MODEL), beta * D_FF ** -0.5)
    ln1_g = 1.0 + nrm(ks[12], (DEPTH, D_MODEL), 0.05)
    ln1_b = nrm(ks[13], (DEPTH, D_MODEL), 0.02)
    ln2_g = 1.0 + nrm(ks[14], (DEPTH, D_MODEL), 0.05)
    ln2_b = nrm(ks[15], (DEPTH, D_MODEL), 0.02)
    return {"x": x, "pool_w": pool_w, "pool_scale": pool_scale, "w_q": w_q, "w_kv": w_kv,
            "w_o": w_o, "ffn_w_gate": ffn_w_gate, "ffn_w_up": ffn_w_up, "ffn_conv_w": ffn_conv_w,
            "ffn_conv_b": ffn_conv_b, "ffn_w_down": ffn_w_down, "ln1_g": ln1_g, "ln1_b": ln1_b,
            "ln2_g": ln2_g, "ln2_b": ln2_b}


def reference(x, pool_w, pool_scale, w_q, w_kv, w_o, ffn_w_gate, ffn_w_up, ffn_conv_w,
              ffn_conv_b, ffn_w_down, ln1_g, ln1_b, ln2_g, ln2_b):
    B, S, _ = x.shape
    cos, sin = rope_tables(S)
    k_shared = None
    v_shared = None
    for i in range(DEPTH):
        if i < N_A_LAYERS:
            mix = pool_mixer(x, pool_w[i], pool_scale[i])
        else:
            j = i - N_A_LAYERS
            mix = dilated_attention(x, k_shared, v_shared, w_q[j], w_o[j], cos, sin)
        x = layer_norm(DEEPNORM_ALPHA * x + mix, ln1_g[i], ln1_b[i])
        ffn = conv_ffn(x, ffn_w_gate[i], ffn_w_up[i], ffn_conv_w[i], ffn_conv_b[i], ffn_w_down[i])
        x = layer_norm(DEEPNORM_ALPHA * x + ffn, ln2_g[i], ln2_b[i])
        if i == N_A_LAYERS - 1:
            kv = (x @ w_kv).reshape(B, S, 2, N_GROUPS * N_HEADS, HEAD_DIM)
            k_shared = apply_rope(kv[:, :, 0], cos, sin).reshape(B, S, N_GROUPS, N_HEADS, HEAD_DIM)
            v_shared = kv[:, :, 1].reshape(B, S, N_GROUPS, N_HEADS, HEAD_DIM)
    return x
```

```python
import functools

import jax
import jax.numpy as jnp
from jax import lax
from jax.experimental import pallas as pl
from jax.experimental.pallas import tpu as pltpu

D_MODEL = 1024
DEPTH = 4
N_A_LAYERS = 2
POOL_WINDOWS = (2, 4, 8, 16)
POOL_GROUP_DIM = D_MODEL // len(POOL_WINDOWS)
POOL_HALO = 16
HEAD_DIM = 64
N_HEADS = 16
DILATED_GROUPS = ((128, 1), (512, 4), (2048, 16))
N_GROUPS = len(DILATED_GROUPS)
DILATIONS = tuple(d for _, d in DILATED_GROUPS)
SPAN = 128
ATTN_DIM = N_HEADS * HEAD_DIM
ROPE_THETA = 10000.0
D_FF = 2816
DEEPNORM_ALPHA = (2.0 * DEPTH) ** 0.25
LN_EPS = 1e-5

LANES = 128
SUBLANES = 8
MXU_DIM = 256
HEADS_PER_BLOCK = LANES // HEAD_DIM
N_HEAD_BLOCKS = ATTN_DIM // LANES
VMEM_LIMIT_BYTES = 56 * 1024 * 1024

FFN_ROWS = 512
FFN_COLS = MXU_DIM
PROJ_ROWS = 512
MASK_VALUE = -1e30

assert all(w // d == SPAN for w, d in DILATED_GROUPS)
assert D_FF % FFN_COLS == 0


def _layer_norm(y, gain, bias):
    mu = jnp.mean(y, axis=-1, keepdims=True)
    yc = y - mu
    var = jnp.mean(yc * yc, axis=-1, keepdims=True)
    return yc * lax.rsqrt(var + LN_EPS) * gain + bias


def _resident(shape):
    return pl.BlockSpec(shape, lambda *_: (0,) * len(shape), pipeline_mode=pl.Buffered(1))


def _conv_ffn_ln(x1, first_tile, wg_ref, wu_ref, cw_ref, cb_ref, wd_ref, g2_ref, b2_ref,
                 gcarry_ref, h_ref, out_ref):
    rows = x1.shape[0]
    x1b = x1.astype(jnp.bfloat16)
    row_id = lax.broadcasted_iota(jnp.int32, (rows, FFN_COLS), 0)
    for c in range(D_FF // FFN_COLS):
        cols = slice(c * FFN_COLS, (c + 1) * FFN_COLS)
        g = jnp.dot(x1b, wg_ref[:, cols], preferred_element_type=jnp.float32)
        u = jnp.dot(x1b, wu_ref[:, cols], preferred_element_type=jnp.float32)
        carry = jnp.where(first_tile, 0.0, gcarry_ref[:, cols])
        gcarry_ref[:, cols] = g[rows - SUBLANES:, :]
        prev1 = carry[SUBLANES - 1:SUBLANES, :]
        prev2 = carry[SUBLANES - 2:SUBLANES - 1, :]
        g1 = jnp.where(row_id == 0, prev1, pltpu.roll(g, 1, 0))
        g2 = jnp.where(row_id == 0, prev2, jnp.where(row_id == 1, prev1, pltpu.roll(g, 2, 0)))
        conv = cb_ref[:, cols] + cw_ref[0:1, cols] * g2
        conv = conv + cw_ref[1:2, cols] * g1
        conv = conv + cw_ref[2:3, cols] * g
        h_ref[:, cols] = (jax.nn.gelu(conv, approximate=True) * u).astype(jnp.bfloat16)
    ffn = jnp.dot(h_ref[...], wd_ref[...], preferred_element_type=jnp.float32)
    out_ref[0] = _layer_norm(DEEPNORM_ALPHA * x1 + ffn, g2_ref[...], b2_ref[...])


def _ffn_pool_kernel(x_ref, pw_ref, ps_ref, g1_ref, b1_ref, wg_ref, wu_ref, cw_ref, cb_ref, wd_ref,
                     g2_ref, b2_ref, out_ref, xcarry_ref, gcarry_ref, h_ref):
    tile = pl.program_id(1)
    first_tile = tile == 0
    x = x_ref[0]
    rows = x.shape[0]
    carry = jnp.where(first_tile, 0.0, xcarry_ref[...])
    xcarry_ref[...] = x[rows - POOL_HALO:, :]
    xe = jnp.concatenate([carry, x], axis=0)
    pos = tile * rows + lax.broadcasted_iota(jnp.int32, (rows, 1), 0)
    mixes = []
    for gi, w in enumerate(POOL_WINDOWS):
        cols = slice(gi * POOL_GROUP_DIM, (gi + 1) * POOL_GROUP_DIM)
        s = xe[:, cols]
        k = 1
        while k < w:
            s = s + pltpu.roll(s, k, 0)
            k *= 2
        count = jnp.minimum(pos + 1, w).astype(jnp.float32)
        pooled = s[POOL_HALO:, :] / count
        diff = (pooled - x[:, cols]).astype(jnp.bfloat16)
        mixes.append(jnp.dot(diff, pw_ref[gi], preferred_element_type=jnp.float32))
    mix = jnp.concatenate(mixes, axis=1) * ps_ref[...]
    x1 = _layer_norm(DEEPNORM_ALPHA * x + mix, g1_ref[...], b1_ref[...])
    _conv_ffn_ln(x1, first_tile, wg_ref, wu_ref, cw_ref, cb_ref, wd_ref, g2_ref, b2_ref,
                 gcarry_ref, h_ref, out_ref)


def _ffn_attn_kernel(x_ref, o_ref, wo_ref, g1_ref, b1_ref, wg_ref, wu_ref, cw_ref, cb_ref, wd_ref,
                     g2_ref, b2_ref, out_ref, gcarry_ref, h_ref):
    first_tile = pl.program_id(1) == 0
    o = jnp.concatenate([o_ref[0, hb] for hb in range(N_HEAD_BLOCKS)], axis=1)
    mix = jnp.dot(o, wo_ref[...], preferred_element_type=jnp.float32)
    x1 = _layer_norm(DEEPNORM_ALPHA * x_ref[0] + mix, g1_ref[...], b1_ref[...])
    _conv_ffn_ln(x1, first_tile, wg_ref, wu_ref, cw_ref, cb_ref, wd_ref, g2_ref, b2_ref,
                 gcarry_ref, h_ref, out_ref)


def _ffn_specs():
    return [_resident((D_MODEL, D_FF)), _resident((D_MODEL, D_FF)), _resident((3, D_FF)),
            _resident((1, D_FF)), _resident((D_FF, D_MODEL)), _resident((1, D_MODEL)),
            _resident((1, D_MODEL))]


def _ffn_scratch():
    return [pltpu.VMEM((SUBLANES, D_FF), jnp.float32), pltpu.VMEM((FFN_ROWS, D_FF), jnp.bfloat16)]


def _row_tile_spec():
    return pl.BlockSpec((1, FFN_ROWS, D_MODEL), lambda b, t: (b, t, 0))


def _ffn_params():
    return pltpu.CompilerParams(dimension_semantics=("arbitrary", "arbitrary"),
                                vmem_limit_bytes=VMEM_LIMIT_BYTES)


def _ffn_pool_layer(x, pw, ps, g1, b1, ffn):
    batch, seq, _ = x.shape
    return pl.pallas_call(
        _ffn_pool_kernel,
        out_shape=jax.ShapeDtypeStruct(x.shape, jnp.float32),
        grid=(batch, seq // FFN_ROWS),
        in_specs=[_row_tile_spec(),
                  _resident((len(POOL_WINDOWS), POOL_GROUP_DIM, POOL_GROUP_DIM)),
                  _resident((1, D_MODEL)), _resident((1, D_MODEL)), _resident((1, D_MODEL))]
        + _ffn_specs(),
        out_specs=_row_tile_spec(),
        scratch_shapes=[pltpu.VMEM((POOL_HALO, D_MODEL), jnp.float32)] + _ffn_scratch(),
        compiler_params=_ffn_params(),
        name="ffn_pool",
    )(x, pw, ps, g1, b1, *ffn)


def _ffn_attn_layer(x, o, wo, g1, b1, ffn):
    batch, seq, _ = x.shape
    o_spec = pl.BlockSpec((1, N_HEAD_BLOCKS, FFN_ROWS, LANES), lambda b, t: (b, 0, t, 0))
    return pl.pallas_call(
        _ffn_attn_kernel,
        out_shape=jax.ShapeDtypeStruct(x.shape, jnp.float32),
        grid=(batch, seq // FFN_ROWS),
        in_specs=[_row_tile_spec(), o_spec, _resident((ATTN_DIM, D_MODEL)),
                  _resident((1, D_MODEL)), _resident((1, D_MODEL))] + _ffn_specs(),
        out_specs=_row_tile_spec(),
        scratch_shapes=_ffn_scratch(),
        compiler_params=_ffn_params(),
        name="ffn_attn",
    )(x, o, wo, g1, b1, *ffn)


def _dilated_rows(xcols_ref, first_row, n_rows, dilation):
    n_blocks, seq, _ = xcols_ref.shape
    seg = seq // dilation
    pieces = []
    done = 0
    while done < n_rows:
        residue, m0 = divmod(first_row + done, seg)
        n = min(n_rows - done, seg - m0)
        rows = pl.ds(m0 * dilation + residue, n, stride=dilation) if dilation > 1 else pl.ds(m0, n)
        pieces.append(jnp.concatenate([xcols_ref[cb, rows, :] for cb in range(n_blocks)], axis=1))
        done += n
    return pieces[0] if len(pieces) == 1 else jnp.concatenate(pieces, axis=0)


def _rotate_half(t, first_half):
    return jnp.where(first_half, pltpu.roll(t, LANES - HEAD_DIM // 2, 1), pltpu.roll(t, HEAD_DIM // 2, 1))


def _proj_kernel(x_ref, cos_ref, sin_ref, *refs, ropes):
    n_out = len(ropes)
    w_refs, out_refs, xcols_ref = refs[:n_out], refs[n_out:2 * n_out], refs[2 * n_out]
    seq = x_ref.shape[1]
    group = pl.program_id(1)

    @pl.when(group == 0)
    def _():
        for cb in range(D_MODEL // LANES):
            xcols_ref[cb] = x_ref[0, :, cb * LANES:(cb + 1) * LANES]

    lane = lax.broadcasted_iota(jnp.int32, (PROJ_ROWS, LANES), 1)
    first_half = (lane % HEAD_DIM) < HEAD_DIM // 2
    for gi, dilation in enumerate(DILATIONS):
        @pl.when(group == gi)
        def _(dilation=dilation):
            for c in range(seq // PROJ_ROWS):
                rows = pl.ds(c * PROJ_ROWS, PROJ_ROWS)
                xb = _dilated_rows(xcols_ref, c * PROJ_ROWS, PROJ_ROWS, dilation).astype(jnp.bfloat16)
                for w_ref, out_ref, rope in zip(w_refs, out_refs, ropes):
                    y = jnp.dot(xb, w_ref[...], preferred_element_type=jnp.float32)
                    for hb in range(N_HEAD_BLOCKS):
                        yh = y[:, hb * LANES:(hb + 1) * LANES]
                        if rope:
                            yh = yh * cos_ref[0, rows, :] + _rotate_half(yh, first_half) * sin_ref[0, rows, :]
                        out_ref[0, 0, hb, rows, :] = yh.astype(jnp.bfloat16)


def _project(x, cos, sin, weights, ropes):
    batch, seq, _ = x.shape
    out_sds = jax.ShapeDtypeStruct((batch, N_GROUPS, N_HEAD_BLOCKS, seq, LANES), jnp.bfloat16)
    table_spec = pl.BlockSpec((1, seq, LANES), lambda b, g: (g, 0, 0))
    outs = pl.pallas_call(
        functools.partial(_proj_kernel, ropes=ropes),
        out_shape=[out_sds] * len(weights),
        grid=(batch, N_GROUPS),
        in_specs=[pl.BlockSpec((1, seq, D_MODEL), lambda b, g: (b, 0, 0)), table_spec, table_spec]
        + [pl.BlockSpec((D_MODEL, ATTN_DIM), lambda b, g: (0, g))] * len(weights),
        out_specs=[pl.BlockSpec((1, 1, N_HEAD_BLOCKS, seq, LANES), lambda b, g: (b, g, 0, 0, 0))] * len(weights),
        scratch_shapes=[pltpu.VMEM((D_MODEL // LANES, seq, LANES), jnp.float32)],
        compiler_params=pltpu.CompilerParams(dimension_semantics=("arbitrary", "arbitrary"),
                                             vmem_limit_bytes=VMEM_LIMIT_BYTES),
        name="proj_" + "".join("r" if r else "p" for r in ropes),
    )(x, cos, sin, *weights)
    return outs


def _attn_kernel(q_ref, k_ref, v_ref, o_ref, acc_ref, m_ref, l_ref):
    seq = q_ref.shape[1]
    lane = lax.broadcasted_iota(jnp.int32, (SPAN, LANES), 1)
    head_lanes = [lane // HEAD_DIM == h for h in range(HEADS_PER_BLOCK)]
    qi = lax.broadcasted_iota(jnp.int32, (SPAN, 2 * SPAN), 0)
    kj = lax.broadcasted_iota(jnp.int32, (SPAN, 2 * SPAN), 1)
    rel = SPAN + qi - kj
    band_bias = jnp.where((rel >= 0) & (rel <= SPAN), 0.0, MASK_VALUE).astype(jnp.float32)
    causal_bias = band_bias[:, SPAN:]

    def aligned(i):
        return i if isinstance(i, int) else pl.multiple_of(i, SPAN)

    def block(gi, seg, j, with_prev):
        dilation = DILATIONS[gi]
        blocks_per_seg = seq // dilation // SPAN
        row0 = aligned((seg * blocks_per_seg + j) * SPAN)
        q = q_ref[gi, pl.ds(row0, SPAN), :]
        if with_prev:
            keys = pl.ds(aligned(row0 - SPAN), 2 * SPAN)
            bias = band_bias
        else:
            keys = pl.ds(row0, SPAN)
            bias = causal_bias
        k = k_ref[gi, keys, :]
        v = v_ref[gi, keys, :]
        acc = m_all = l_all = None
        for h in range(HEADS_PER_BLOCK):
            qh = jnp.where(head_lanes[h], q, jnp.zeros_like(q))
            s = lax.dot_general(qh, k, (((1,), (1,)), ((), ())), preferred_element_type=jnp.float32)
            s = s + bias
            m = jnp.max(s, axis=-1, keepdims=True)
            p = jnp.exp(s - m)
            l = jnp.sum(p, axis=-1, keepdims=True)
            pv = jnp.dot(p.astype(jnp.bfloat16), v, preferred_element_type=jnp.float32)
            mb = jnp.broadcast_to(m, (SPAN, LANES))
            lb = jnp.broadcast_to(l, (SPAN, LANES))
            if h == 0:
                acc, m_all, l_all = pv, mb, lb
            else:
                acc = jnp.where(head_lanes[h], pv, acc)
                m_all = jnp.where(head_lanes[h], mb, m_all)
                l_all = jnp.where(head_lanes[h], lb, l_all)
        if dilation == 1:
            dst = pl.ds(row0, SPAN)
        else:
            dst = pl.ds(j * SPAN * dilation + seg, SPAN, stride=dilation)
        acc_ref[gi, dst, :] = acc
        m_ref[gi, dst, :] = m_all
        l_ref[gi, dst, :] = l_all

    for gi, dilation in enumerate(DILATIONS):
        blocks_per_seg = seq // dilation // SPAN
        n_segs = seq // SPAN // blocks_per_seg

        def segment(seg, carry, gi=gi, blocks_per_seg=blocks_per_seg):
            block(gi, seg, 0, with_prev=False)
            if blocks_per_seg > 1:
                def inner(j, c):
                    block(gi, seg, j, with_prev=True)
                    return c
                lax.fori_loop(1, blocks_per_seg, inner, 0)
            return carry

        if n_segs == 1:
            segment(0, 0)
        else:
            lax.fori_loop(0, n_segs, segment, 0)

    merge_rows = 256
    for c in range(seq // merge_rows):
        rows = pl.ds(c * merge_rows, merge_rows)
        ms = [m_ref[gi, rows, :] for gi in range(N_GROUPS)]
        m_max = functools.reduce(jnp.maximum, ms)
        es = [jnp.exp(m - m_max) for m in ms]
        num = sum(e * acc_ref[gi, rows, :] for gi, e in enumerate(es))
        den = sum(e * l_ref[gi, rows, :] for gi, e in enumerate(es))
        o_ref[rows, :] = (num / den).astype(jnp.bfloat16)


def _attention(q, k, v):
    batch, _, _, seq, _ = q.shape
    qkv_spec = pl.BlockSpec((None, N_GROUPS, None, seq, LANES), lambda b, hb: (b, 0, hb, 0, 0))
    return pl.pallas_call(
        _attn_kernel,
        out_shape=jax.ShapeDtypeStruct((batch, N_HEAD_BLOCKS, seq, LANES), jnp.bfloat16),
        grid=(batch, N_HEAD_BLOCKS),
        in_specs=[qkv_spec, qkv_spec, qkv_spec],
        out_specs=pl.BlockSpec((None, None, seq, LANES), lambda b, hb: (b, hb, 0, 0)),
        scratch_shapes=[pltpu.VMEM((N_GROUPS, seq, LANES), jnp.float32)] * 3,
        compiler_params=pltpu.CompilerParams(dimension_semantics=("arbitrary", "arbitrary"),
                                             vmem_limit_bytes=VMEM_LIMIT_BYTES),
        name="dilated_attn",
    )(q, k, v)


def _rope_tables(seq, scale):
    inv_freq = ROPE_THETA ** (-jnp.arange(0, HEAD_DIM, 2, dtype=jnp.float32) / HEAD_DIM)
    lane = jnp.arange(LANES)
    freq = inv_freq[lane % (HEAD_DIM // 2)]
    sign = jnp.where((lane % HEAD_DIM) < HEAD_DIM // 2, -1.0, 1.0).astype(jnp.float32)
    row = jnp.arange(seq)
    cos, sin = [], []
    for dilation in DILATIONS:
        seg = seq // dilation
        pos = ((row % seg) * dilation + row // seg).astype(jnp.float32)
        ang = pos[:, None] * freq[None, :]
        cos.append(jnp.cos(ang) * scale)
        sin.append(jnp.sin(ang) * sign[None, :] * scale)
    return jnp.stack(cos), jnp.stack(sin)


def kernel(x, pool_w, pool_scale, w_q, w_kv, w_o, ffn_w_gate, ffn_w_up, ffn_conv_w, ffn_conv_b,
           ffn_w_down, ln1_g, ln1_b, ln2_g, ln2_b):
    batch, seq, d_model = x.shape
    assert d_model == D_MODEL and seq % (max(DILATIONS) * SPAN) == 0 and seq % FFN_ROWS == 0
    bf16 = jnp.bfloat16
    q_dim = N_GROUPS * ATTN_DIM
    row = lambda a: a.reshape(1, -1)
    cos_q, sin_q = _rope_tables(seq, HEAD_DIM ** -0.5)
    cos_k, sin_k = _rope_tables(seq, 1.0)
    k = v = None
    for i in range(DEPTH):
        ffn = (ffn_w_gate[i].astype(bf16), ffn_w_up[i].astype(bf16), ffn_conv_w[i], row(ffn_conv_b[i]),
               ffn_w_down[i].astype(bf16), row(ln2_g[i]), row(ln2_b[i]))
        if i < N_A_LAYERS:
            x = _ffn_pool_layer(x, pool_w[i].astype(bf16), row(pool_scale[i]), row(ln1_g[i]), row(ln1_b[i]), ffn)
        else:
            j = i - N_A_LAYERS
            (q,) = _project(x, cos_q, sin_q, [w_q[j].astype(bf16)], (True,))
            o = _attention(q, k, v)
            x = _ffn_attn_layer(x, o, w_o[j].astype(bf16), row(ln1_g[i]), row(ln1_b[i]), ffn)
        if i == N_A_LAYERS - 1:
            k, v = _project(x, cos_k, sin_k,
                            [w_kv[:, :q_dim].astype(bf16), w_kv[:, q_dim:].astype(bf16)], (True, False))
    return x
```

```python
import functools

import jax
import jax.numpy as jnp
from jax import lax
from jax.experimental import pallas as pl
from jax.experimental.pallas import tpu as pltpu

D_MODEL = 1024
DEPTH = 4
N_A_LAYERS = 2
POOL_WINDOWS = (2, 4, 8, 16)
POOL_GROUP_DIM = D_MODEL // len(POOL_WINDOWS)
POOL_HALO = 16
HEAD_DIM = 64
N_HEADS = 16
DILATED_GROUPS = ((128, 1), (512, 4), (2048, 16))
N_GROUPS = len(DILATED_GROUPS)
DILATIONS = tuple(d for _, d in DILATED_GROUPS)
SPAN = 128
ATTN_DIM = N_HEADS * HEAD_DIM
ROPE_THETA = 10000.0
D_FF = 2816
DEEPNORM_ALPHA = (2.0 * DEPTH) ** 0.25
LN_EPS = 1e-5

LANES = 128
SUBLANES = 8
MXU_DIM = 256
HEADS_PER_BLOCK = LANES // HEAD_DIM
N_HEAD_BLOCKS = ATTN_DIM // LANES
VMEM_LIMIT_BYTES = 56 * 1024 * 1024

FFN_ROWS = 512
FFN_COLS = MXU_DIM
PROJ_ROWS = 512
MASK_VALUE = -1e30
LOG2_E = 1.4426950408889634
ATTN_LOOKAHEAD = 3

assert all(w // d == SPAN for w, d in DILATED_GROUPS)
assert D_FF % FFN_COLS == 0


def _layer_norm(y, gain, bias):
    mu = jnp.mean(y, axis=-1, keepdims=True)
    yc = y - mu
    var = jnp.mean(yc * yc, axis=-1, keepdims=True)
    return yc * lax.rsqrt(var + LN_EPS) * gain + bias


def _resident(shape):
    return pl.BlockSpec(shape, lambda *_: (0,) * len(shape), pipeline_mode=pl.Buffered(1))


def _conv_ffn_ln(x1, first_tile, wg_ref, wu_ref, cw_ref, cb_ref, wd_ref, g2_ref, b2_ref,
                 gcarry_ref, h_ref, out_ref):
    rows = x1.shape[0]
    x1b = x1.astype(jnp.bfloat16)
    row_id = lax.broadcasted_iota(jnp.int32, (rows, FFN_COLS), 0)
    for c in range(D_FF // FFN_COLS):
        cols = slice(c * FFN_COLS, (c + 1) * FFN_COLS)
        g = jnp.dot(x1b, wg_ref[:, cols], preferred_element_type=jnp.float32)
        u = jnp.dot(x1b, wu_ref[:, cols], preferred_element_type=jnp.float32)
        carry = jnp.where(first_tile, 0.0, gcarry_ref[:, cols])
        gcarry_ref[:, cols] = g[rows - SUBLANES:, :]
        prev1 = carry[SUBLANES - 1:SUBLANES, :]
        prev2 = carry[SUBLANES - 2:SUBLANES - 1, :]
        g1 = jnp.where(row_id == 0, prev1, pltpu.roll(g, 1, 0))
        g2 = jnp.where(row_id == 0, prev2, jnp.where(row_id == 1, prev1, pltpu.roll(g, 2, 0)))
        conv = cb_ref[:, cols] + cw_ref[0:1, cols] * g2
        conv = conv + cw_ref[1:2, cols] * g1
        conv = conv + cw_ref[2:3, cols] * g
        h_ref[:, cols] = (jax.nn.gelu(conv, approximate=True) * u).astype(jnp.bfloat16)
    ffn = jnp.dot(h_ref[...], wd_ref[...], preferred_element_type=jnp.float32)
    out_ref[0] = _layer_norm(DEEPNORM_ALPHA * x1 + ffn, g2_ref[...], b2_ref[...])


def _ffn_pool_kernel(x_ref, pw_ref, ps_ref, g1_ref, b1_ref, wg_ref, wu_ref, cw_ref, cb_ref, wd_ref,
                     g2_ref, b2_ref, out_ref, xcarry_ref, gcarry_ref, h_ref):
    tile = pl.program_id(1)
    first_tile = tile == 0
    x = x_ref[0]
    rows = x.shape[0]
    carry = jnp.where(first_tile, 0.0, xcarry_ref[...])
    xcarry_ref[...] = x[rows - POOL_HALO:, :]
    xe = jnp.concatenate([carry, x], axis=0)
    pos = tile * rows + lax.broadcasted_iota(jnp.int32, (rows, 1), 0)
    mixes = []
    for gi, w in enumerate(POOL_WINDOWS):
        cols = slice(gi * POOL_GROUP_DIM, (gi + 1) * POOL_GROUP_DIM)
        s = xe[:, cols]
        k = 1
        while k < w:
            s = s + pltpu.roll(s, k, 0)
            k *= 2
        count = jnp.minimum(pos + 1, w).astype(jnp.float32)
        pooled = s[POOL_HALO:, :] / count
        diff = (pooled - x[:, cols]).astype(jnp.bfloat16)
        mixes.append(jnp.dot(diff, pw_ref[gi], preferred_element_type=jnp.float32))
    mix = jnp.concatenate(mixes, axis=1) * ps_ref[...]
    x1 = _layer_norm(DEEPNORM_ALPHA * x + mix, g1_ref[...], b1_ref[...])
    _conv_ffn_ln(x1, first_tile, wg_ref, wu_ref, cw_ref, cb_ref, wd_ref, g2_ref, b2_ref,
                 gcarry_ref, h_ref, out_ref)


def _ffn_attn_kernel(x_ref, o_ref, wo_ref, g1_ref, b1_ref, wg_ref, wu_ref, cw_ref, cb_ref, wd_ref,
                     g2_ref, b2_ref, out_ref, gcarry_ref, h_ref):
    first_tile = pl.program_id(1) == 0
    o = jnp.concatenate([o_ref[0, hb] for hb in range(N_HEAD_BLOCKS)], axis=1)
    mix = jnp.dot(o, wo_ref[...], preferred_element_type=jnp.float32)
    x1 = _layer_norm(DEEPNORM_ALPHA * x_ref[0] + mix, g1_ref[...], b1_ref[...])
    _conv_ffn_ln(x1, first_tile, wg_ref, wu_ref, cw_ref, cb_ref, wd_ref, g2_ref, b2_ref,
                 gcarry_ref, h_ref, out_ref)


def _ffn_specs():
    return [_resident((D_MODEL, D_FF)), _resident((D_MODEL, D_FF)), _resident((3, D_FF)),
            _resident((1, D_FF)), _resident((D_FF, D_MODEL)), _resident((1, D_MODEL)),
            _resident((1, D_MODEL))]


def _ffn_scratch():
    return [pltpu.VMEM((SUBLANES, D_FF), jnp.float32), pltpu.VMEM((FFN_ROWS, D_FF), jnp.bfloat16)]


def _row_tile_spec():
    return pl.BlockSpec((1, FFN_ROWS, D_MODEL), lambda b, t: (b, t, 0))


def _ffn_params():
    return pltpu.CompilerParams(dimension_semantics=("arbitrary", "arbitrary"),
                                vmem_limit_bytes=VMEM_LIMIT_BYTES)


def _ffn_pool_layer(x, pw, ps, g1, b1, ffn):
    batch, seq, _ = x.shape
    return pl.pallas_call(
        _ffn_pool_kernel,
        out_shape=jax.ShapeDtypeStruct(x.shape, jnp.float32),
        grid=(batch, seq // FFN_ROWS),
        in_specs=[_row_tile_spec(),
                  _resident((len(POOL_WINDOWS), POOL_GROUP_DIM, POOL_GROUP_DIM)),
                  _resident((1, D_MODEL)), _resident((1, D_MODEL)), _resident((1, D_MODEL))]
        + _ffn_specs(),
        out_specs=_row_tile_spec(),
        scratch_shapes=[pltpu.VMEM((POOL_HALO, D_MODEL), jnp.float32)] + _ffn_scratch(),
        compiler_params=_ffn_params(),
        name="ffn_pool",
    )(x, pw, ps, g1, b1, *ffn)


def _ffn_attn_layer(x, o, wo, g1, b1, ffn):
    batch, seq, _ = x.shape
    o_spec = pl.BlockSpec((1, N_HEAD_BLOCKS, FFN_ROWS, LANES), lambda b, t: (b, 0, t, 0))
    return pl.pallas_call(
        _ffn_attn_kernel,
        out_shape=jax.ShapeDtypeStruct(x.shape, jnp.float32),
        grid=(batch, seq // FFN_ROWS),
        in_specs=[_row_tile_spec(), o_spec, _resident((ATTN_DIM, D_MODEL)),
                  _resident((1, D_MODEL)), _resident((1, D_MODEL))] + _ffn_specs(),
        out_specs=_row_tile_spec(),
        scratch_shapes=_ffn_scratch(),
        compiler_params=_ffn_params(),
        name="ffn_attn",
    )(x, o, wo, g1, b1, *ffn)


def _dilated_rows(xcols_ref, first_row, n_rows, dilation):
    n_blocks, seq, _ = xcols_ref.shape
    seg = seq // dilation
    pieces = []
    done = 0
    while done < n_rows:
        residue, m0 = divmod(first_row + done, seg)
        n = min(n_rows - done, seg - m0)
        rows = pl.ds(m0 * dilation + residue, n, stride=dilation) if dilation > 1 else pl.ds(m0, n)
        pieces.append(jnp.concatenate([xcols_ref[cb, rows, :] for cb in range(n_blocks)], axis=1))
        done += n
    return pieces[0] if len(pieces) == 1 else jnp.concatenate(pieces, axis=0)


def _rotate_half(t, first_half):
    return jnp.where(first_half, pltpu.roll(t, LANES - HEAD_DIM // 2, 1), pltpu.roll(t, HEAD_DIM // 2, 1))


def _proj_kernel(x_ref, cos_ref, sin_ref, *refs, ropes):
    n_out = len(ropes)
    w_refs, out_refs, xcols_ref = refs[:n_out], refs[n_out:2 * n_out], refs[2 * n_out]
    seq = x_ref.shape[1]
    group = pl.program_id(1)

    @pl.when(group == 0)
    def _():
        for cb in range(D_MODEL // LANES):
            xcols_ref[cb] = x_ref[0, :, cb * LANES:(cb + 1) * LANES]

    lane = lax.broadcasted_iota(jnp.int32, (PROJ_ROWS, LANES), 1)
    first_half = (lane % HEAD_DIM) < HEAD_DIM // 2
    for gi, dilation in enumerate(DILATIONS):
        @pl.when(group == gi)
        def _(dilation=dilation):
            for c in range(seq // PROJ_ROWS):
                rows = pl.ds(c * PROJ_ROWS, PROJ_ROWS)
                xb = _dilated_rows(xcols_ref, c * PROJ_ROWS, PROJ_ROWS, dilation).astype(jnp.bfloat16)
                for w_ref, out_ref, rope in zip(w_refs, out_refs, ropes):
                    y = jnp.dot(xb, w_ref[...], preferred_element_type=jnp.float32)
                    for hb in range(N_HEAD_BLOCKS):
                        yh = y[:, hb * LANES:(hb + 1) * LANES]
                        if rope:
                            yh = yh * cos_ref[0, rows, :] + _rotate_half(yh, first_half) * sin_ref[0, rows, :]
                        out_ref[0, 0, hb, rows, :] = yh.astype(jnp.bfloat16)


def _project(x, cos, sin, weights, ropes):
    batch, seq, _ = x.shape
    out_sds = jax.ShapeDtypeStruct((batch, N_GROUPS, N_HEAD_BLOCKS, seq, LANES), jnp.bfloat16)
    table_spec = pl.BlockSpec((1, seq, LANES), lambda b, g: (g, 0, 0))
    outs = pl.pallas_call(
        functools.partial(_proj_kernel, ropes=ropes),
        out_shape=[out_sds] * len(weights),
        grid=(batch, N_GROUPS),
        in_specs=[pl.BlockSpec((1, seq, D_MODEL), lambda b, g: (b, 0, 0)), table_spec, table_spec]
        + [pl.BlockSpec((D_MODEL, ATTN_DIM), lambda b, g: (0, g))] * len(weights),
        out_specs=[pl.BlockSpec((1, 1, N_HEAD_BLOCKS, seq, LANES), lambda b, g: (b, g, 0, 0, 0))] * len(weights),
        scratch_shapes=[pltpu.VMEM((D_MODEL // LANES, seq, LANES), jnp.float32)],
        compiler_params=pltpu.CompilerParams(dimension_semantics=("arbitrary", "arbitrary"),
                                             vmem_limit_bytes=VMEM_LIMIT_BYTES),
        name="proj_" + "".join("r" if r else "p" for r in ropes),
    )(x, cos, sin, *weights)
    return outs


def _attn_kernel(q_ref, k_ref, v_ref, o_ref, acc_ref, m_ref, l_ref):
    assert HEADS_PER_BLOCK == 2
    seq = q_ref.shape[1]

    def head_masks(rows, dtype):
        head = (lax.broadcasted_iota(jnp.int32, (rows, LANES), 1) // HEAD_DIM).astype(dtype)
        return [head == h for h in range(HEADS_PER_BLOCK)]

    head_lanes = head_masks(SPAN, jnp.float32)
    head_lanes_b = head_masks(SPAN, jnp.bfloat16)
    head_lanes_b2 = head_masks(2 * SPAN, jnp.bfloat16)
    qi = lax.broadcasted_iota(jnp.int32, (SPAN, 2 * SPAN), 0)
    kj = lax.broadcasted_iota(jnp.int32, (SPAN, 2 * SPAN), 1)
    rel = SPAN + qi - kj
    band_bias = jnp.where((rel >= 0) & (rel <= SPAN), 0.0, MASK_VALUE).astype(jnp.float32)
    causal_bias = band_bias[:, SPAN:]

    def key_rows(gi, seg, j):
        blocks_per_seg = seq // DILATIONS[gi] // SPAN
        row0 = (seg * blocks_per_seg + j) * SPAN
        return row0, (pl.ds(row0 - SPAN, 2 * SPAN) if j > 0 else pl.ds(row0, SPAN))

    def scores(gi, seg, j):
        row0, keys = key_rows(gi, seg, j)
        q = q_ref[gi, pl.ds(row0, SPAN), :]
        k = k_ref[gi, keys, :]
        bias = band_bias if j > 0 else causal_bias
        out = []
        for h in range(HEADS_PER_BLOCK):
            qh = jnp.where(head_lanes_b[h], q, jnp.zeros_like(q))
            s = lax.dot_general(qh, k, (((1,), (1,)), ((), ())), preferred_element_type=jnp.float32)
            out.append(s + bias)
        return out

    def finish(gi, seg, j, ss):
        row0, keys = key_rows(gi, seg, j)
        v = v_ref[gi, keys, :]
        pvs, mbs = [], []
        for h, s in enumerate(ss):
            m = jnp.max(s, axis=-1, keepdims=True)
            p = jnp.exp2(s - m).astype(jnp.bfloat16)
            vh = jnp.where((head_lanes_b2 if j > 0 else head_lanes_b)[h], v, jnp.ones_like(v))
            pvs.append(jnp.dot(p, vh, preferred_element_type=jnp.float32))
            mbs.append(jnp.broadcast_to(m, (SPAN, LANES)))
        acc = jnp.where(head_lanes[1], pvs[1], pvs[0])
        l_swapped = jnp.where(head_lanes[1], pvs[0], pvs[1])
        m_all = jnp.where(head_lanes[1], mbs[1], mbs[0])
        dilation = DILATIONS[gi]
        if dilation == 1:
            dst = pl.ds(row0, SPAN)
        else:
            dst = pl.ds(j * SPAN * dilation + seg, SPAN, stride=dilation)
        acc_ref[gi, dst, :] = acc
        m_ref[gi, dst, :] = m_all
        l_ref[gi, dst, :] = l_swapped

    blocks = [(gi, seg, j)
              for gi, dilation in enumerate(DILATIONS)
              for seg in range(dilation)
              for j in range(seq // dilation // SPAN)]
    pending = [scores(*b) for b in blocks[:ATTN_LOOKAHEAD]]
    for i, b in enumerate(blocks):
        if i + ATTN_LOOKAHEAD < len(blocks):
            pending.append(scores(*blocks[i + ATTN_LOOKAHEAD]))
        finish(*b, pending.pop(0))

    merge_rows = 256
    for c in range(seq // merge_rows):
        rows = pl.ds(c * merge_rows, merge_rows)
        ms = [m_ref[gi, rows, :] for gi in range(N_GROUPS)]
        m_max = functools.reduce(jnp.maximum, ms)
        es = [jnp.exp2(m - m_max) for m in ms]
        num = sum(e * acc_ref[gi, rows, :] for gi, e in enumerate(es))
        den = sum(e * pltpu.roll(l_ref[gi, rows, :], HEAD_DIM, 1) for gi, e in enumerate(es))
        o_ref[rows, :] = (num / den).astype(jnp.bfloat16)


def _attention(q, k, v):
    batch, _, _, seq, _ = q.shape
    qkv_spec = pl.BlockSpec((None, N_GROUPS, None, seq, LANES), lambda b, hb: (b, 0, hb, 0, 0))
    return pl.pallas_call(
        _attn_kernel,
        out_shape=jax.ShapeDtypeStruct((batch, N_HEAD_BLOCKS, seq, LANES), jnp.bfloat16),
        grid=(batch, N_HEAD_BLOCKS),
        in_specs=[qkv_spec, qkv_spec, qkv_spec],
        out_specs=pl.BlockSpec((None, None, seq, LANES), lambda b, hb: (b, hb, 0, 0)),
        scratch_shapes=[pltpu.VMEM((N_GROUPS, seq, LANES), jnp.float32)] * 3,
        compiler_params=pltpu.CompilerParams(dimension_semantics=("arbitrary", "arbitrary"),
                                             vmem_limit_bytes=VMEM_LIMIT_BYTES),
        name="dilated_attn",
    )(q, k, v)


def _rope_tables(seq, scale):
    inv_freq = ROPE_THETA ** (-jnp.arange(0, HEAD_DIM, 2, dtype=jnp.float32) / HEAD_DIM)
    lane = jnp.arange(LANES)
    freq = inv_freq[lane % (HEAD_DIM // 2)]
    sign = jnp.where((lane % HEAD_DIM) < HEAD_DIM // 2, -1.0, 1.0).astype(jnp.float32)
    row = jnp.arange(seq)
    cos, sin = [], []
    for dilation in DILATIONS:
        seg = seq // dilation
        pos = ((row % seg) * dilation + row // seg).astype(jnp.float32)
        ang = pos[:, None] * freq[None, :]
        cos.append(jnp.cos(ang) * scale)
        sin.append(jnp.sin(ang) * sign[None, :] * scale)
    return jnp.stack(cos), jnp.stack(sin)


def kernel(x, pool_w, pool_scale, w_q, w_kv, w_o, ffn_w_gate, ffn_w_up, ffn_conv_w, ffn_conv_b,
           ffn_w_down, ln1_g, ln1_b, ln2_g, ln2_b):
    batch, seq, d_model = x.shape
    assert d_model == D_MODEL and seq % (max(DILATIONS) * SPAN) == 0 and seq % FFN_ROWS == 0
    bf16 = jnp.bfloat16
    q_dim = N_GROUPS * ATTN_DIM
    row = lambda a: a.reshape(1, -1)
    cos_q, sin_q = _rope_tables(seq, HEAD_DIM ** -0.5 * LOG2_E)
    cos_k, sin_k = _rope_tables(seq, 1.0)
    k = v = None
    for i in range(DEPTH):
        ffn = (ffn_w_gate[i].astype(bf16), ffn_w_up[i].astype(bf16), ffn_conv_w[i], row(ffn_conv_b[i]),
               ffn_w_down[i].astype(bf16), row(ln2_g[i]), row(ln2_b[i]))
        if i < N_A_LAYERS:
            x = _ffn_pool_layer(x, pool_w[i].astype(bf16), row(pool_scale[i]), row(ln1_g[i]), row(ln1_b[i]), ffn)
        else:
            j = i - N_A_LAYERS
            (q,) = _project(x, cos_q, sin_q, [w_q[j].astype(bf16)], (True,))
            o = _attention(q, k, v)
            x = _ffn_attn_layer(x, o, w_o[j].astype(bf16), row(ln1_g[i]), row(ln1_b[i]), ffn)
        if i == N_A_LAYERS - 1:
            k, v = _project(x, cos_k, sin_k,
                            [w_kv[:, :q_dim].astype(bf16), w_kv[:, q_dim:].astype(bf16)], (True, False))
    return x
```

```python
import functools

import jax
import jax.numpy as jnp
from jax import lax
from jax.experimental import pallas as pl
from jax.experimental.pallas import tpu as pltpu

D_MODEL = 1024
DEPTH = 4
N_A_LAYERS = 2
POOL_WINDOWS = (2, 4, 8, 16)
POOL_GROUP_DIM = D_MODEL // len(POOL_WINDOWS)
POOL_HALO = 16
HEAD_DIM = 64
N_HEADS = 16
DILATED_GROUPS = ((128, 1), (512, 4), (2048, 16))
N_GROUPS = len(DILATED_GROUPS)
DILATIONS = tuple(d for _, d in DILATED_GROUPS)
SPAN = 128
ATTN_DIM = N_HEADS * HEAD_DIM
ROPE_THETA = 10000.0
D_FF = 2816
DEEPNORM_ALPHA = (2.0 * DEPTH) ** 0.25
LN_EPS = 1e-5

LANES = 128
SUBLANES = 8
MXU_DIM = 256
HEADS_PER_BLOCK = LANES // HEAD_DIM
N_HEAD_BLOCKS = ATTN_DIM // LANES
VMEM_LIMIT_BYTES = 56 * 1024 * 1024

FFN_ROWS = 512
FFN_COLS = MXU_DIM
PROJ_ROWS = 512
MASK_VALUE = -1e30
LOG2_E = 1.4426950408889634
ATTN_LOOKAHEAD = 3

assert all(w // d == SPAN for w, d in DILATED_GROUPS)
assert D_FF % FFN_COLS == 0


def _layer_norm(y, gain, bias):
    mu = jnp.mean(y, axis=-1, keepdims=True)
    yc = y - mu
    var = jnp.mean(yc * yc, axis=-1, keepdims=True)
    return yc * lax.rsqrt(var + LN_EPS) * gain + bias


def _resident(shape):
    return pl.BlockSpec(shape, lambda *_: (0,) * len(shape), pipeline_mode=pl.Buffered(1))


def _conv_ffn_ln(x1, first_tile, wg_ref, wu_ref, cw_ref, cb_ref, wd_ref, g2_ref, b2_ref,
                 gcarry_ref, h_ref, out_refs):
    rows = x1.shape[0]
    x1b = x1.astype(jnp.bfloat16)
    for c in range(D_FF // FFN_COLS):
        cols = slice(c * FFN_COLS, (c + 1) * FFN_COLS)
        g = jnp.dot(x1b, wg_ref[:, cols], preferred_element_type=jnp.float32)
        u = jnp.dot(x1b, wu_ref[:, cols], preferred_element_type=jnp.float32)
        carry = jnp.where(first_tile, 0.0, gcarry_ref[:, cols])
        gcarry_ref[:, cols] = g[rows - SUBLANES:, :]
        ge = jnp.concatenate([carry, g], axis=0)
        g1 = pltpu.roll(ge, 1, 0)[SUBLANES:, :]
        g2 = pltpu.roll(ge, 2, 0)[SUBLANES:, :]
        conv = cb_ref[:, cols] + cw_ref[0:1, cols] * g2
        conv = conv + cw_ref[1:2, cols] * g1
        conv = conv + cw_ref[2:3, cols] * g
        h_ref[:, cols] = (jax.nn.gelu(conv, approximate=True) * u).astype(jnp.bfloat16)
    ffn = jnp.dot(h_ref[...], wd_ref[...], preferred_element_type=jnp.float32)
    y = _layer_norm(DEEPNORM_ALPHA * x1 + ffn, g2_ref[...], b2_ref[...])
    for out_ref in out_refs:
        out_ref[0] = y.astype(out_ref.dtype)


def _ffn_pool_kernel(x_ref, pw_ref, ps_ref, g1_ref, b1_ref, wg_ref, wu_ref, cw_ref, cb_ref, wd_ref,
                     g2_ref, b2_ref, *refs):
    out_refs, (xcarry_ref, gcarry_ref, h_ref) = refs[:-3], refs[-3:]
    tile = pl.program_id(1)
    first_tile = tile == 0
    x = x_ref[0]
    rows = x.shape[0]
    carry = jnp.where(first_tile, 0.0, xcarry_ref[...])
    xcarry_ref[...] = x[rows - POOL_HALO:, :]
    xe = jnp.concatenate([carry, x], axis=0)
    pos = tile * rows + lax.broadcasted_iota(jnp.int32, (rows, 1), 0)
    mixes = []
    for gi, w in enumerate(POOL_WINDOWS):
        cols = slice(gi * POOL_GROUP_DIM, (gi + 1) * POOL_GROUP_DIM)
        s = xe[:, cols]
        k = 1
        while k < w:
            s = s + pltpu.roll(s, k, 0)
            k *= 2
        inv_count = 1.0 / jnp.minimum(pos + 1, w).astype(jnp.float32)
        pooled = s[POOL_HALO:, :] * inv_count
        diff = (pooled - x[:, cols]).astype(jnp.bfloat16)
        mixes.append(jnp.dot(diff, pw_ref[gi], preferred_element_type=jnp.float32))
    mix = jnp.concatenate(mixes, axis=1) * ps_ref[...]
    x1 = _layer_norm(DEEPNORM_ALPHA * x + mix, g1_ref[...], b1_ref[...])
    _conv_ffn_ln(x1, first_tile, wg_ref, wu_ref, cw_ref, cb_ref, wd_ref, g2_ref, b2_ref,
                 gcarry_ref, h_ref, out_refs)


def _ffn_attn_kernel(x_ref, o_ref, wo_ref, g1_ref, b1_ref, wg_ref, wu_ref, cw_ref, cb_ref, wd_ref,
                     g2_ref, b2_ref, *refs):
    out_refs, (gcarry_ref, h_ref) = refs[:-2], refs[-2:]
    first_tile = pl.program_id(1) == 0
    o = jnp.concatenate([o_ref[0, hb] for hb in range(N_HEAD_BLOCKS)], axis=1)
    mix = jnp.dot(o, wo_ref[...], preferred_element_type=jnp.float32)
    x1 = _layer_norm(DEEPNORM_ALPHA * x_ref[0] + mix, g1_ref[...], b1_ref[...])
    _conv_ffn_ln(x1, first_tile, wg_ref, wu_ref, cw_ref, cb_ref, wd_ref, g2_ref, b2_ref,
                 gcarry_ref, h_ref, out_refs)


def _ffn_specs():
    return [_resident((D_MODEL, D_FF)), _resident((D_MODEL, D_FF)), _resident((3, D_FF)),
            _resident((1, D_FF)), _resident((D_FF, D_MODEL)), _resident((1, D_MODEL)),
            _resident((1, D_MODEL))]


def _ffn_scratch():
    return [pltpu.VMEM((SUBLANES, D_FF), jnp.float32), pltpu.VMEM((FFN_ROWS, D_FF), jnp.bfloat16)]


def _row_tile_spec():
    return pl.BlockSpec((1, FFN_ROWS, D_MODEL), lambda b, t: (b, t, 0))


def _ffn_params():
    return pltpu.CompilerParams(dimension_semantics=("arbitrary", "arbitrary"),
                                vmem_limit_bytes=VMEM_LIMIT_BYTES)


def _ffn_out(x, with_bf16):
    dtypes = (jnp.float32, jnp.bfloat16) if with_bf16 else (jnp.float32,)
    return [jax.ShapeDtypeStruct(x.shape, dt) for dt in dtypes], [_row_tile_spec() for _ in dtypes]


def _ffn_pool_layer(x, pw, ps, g1, b1, ffn, with_bf16):
    batch, seq, _ = x.shape
    out_shape, out_specs = _ffn_out(x, with_bf16)
    return pl.pallas_call(
        _ffn_pool_kernel,
        out_shape=out_shape,
        grid=(batch, seq // FFN_ROWS),
        in_specs=[_row_tile_spec(),
                  _resident((len(POOL_WINDOWS), POOL_GROUP_DIM, POOL_GROUP_DIM)),
                  _resident((1, D_MODEL)), _resident((1, D_MODEL)), _resident((1, D_MODEL))]
        + _ffn_specs(),
        out_specs=out_specs,
        scratch_shapes=[pltpu.VMEM((POOL_HALO, D_MODEL), jnp.float32)] + _ffn_scratch(),
        compiler_params=_ffn_params(),
        name="ffn_pool",
    )(x, pw, ps, g1, b1, *ffn)


def _ffn_attn_layer(x, o, wo, g1, b1, ffn, with_bf16):
    batch, seq, _ = x.shape
    o_spec = pl.BlockSpec((1, N_HEAD_BLOCKS, FFN_ROWS, LANES), lambda b, t: (b, 0, t, 0))
    out_shape, out_specs = _ffn_out(x, with_bf16)
    return pl.pallas_call(
        _ffn_attn_kernel,
        out_shape=out_shape,
        grid=(batch, seq // FFN_ROWS),
        in_specs=[_row_tile_spec(), o_spec, _resident((ATTN_DIM, D_MODEL)),
                  _resident((1, D_MODEL)), _resident((1, D_MODEL))] + _ffn_specs(),
        out_specs=out_specs,
        scratch_shapes=_ffn_scratch(),
        compiler_params=_ffn_params(),
        name="ffn_attn",
    )(x, o, wo, g1, b1, *ffn)


def _rotate_half(t, first_half):
    return jnp.where(first_half, pltpu.roll(t, LANES - HEAD_DIM // 2, 1), pltpu.roll(t, HEAD_DIM // 2, 1))


def _proj_kernel(x_ref, cos_ref, sin_ref, *refs, ropes):
    n_out = len(ropes)
    w_refs, out_refs = refs[:n_out], refs[n_out:]
    seg = x_ref.shape[1]
    lane = lax.broadcasted_iota(jnp.int32, (PROJ_ROWS, LANES), 1)
    first_half = (lane % HEAD_DIM) < HEAD_DIM // 2
    for c in range(cos_ref.shape[1] // PROJ_ROWS):
        pieces = []
        for row in range(c * PROJ_ROWS, (c + 1) * PROJ_ROWS, min(seg, PROJ_ROWS)):
            residue, m0 = divmod(row, seg)
            pieces.append(x_ref[0, pl.ds(m0, min(seg, PROJ_ROWS)), residue * D_MODEL:(residue + 1) * D_MODEL])
        xb = pieces[0] if len(pieces) == 1 else jnp.concatenate(pieces, axis=0)
        rows = pl.ds(c * PROJ_ROWS, PROJ_ROWS)
        for w_ref, out_ref, rope in zip(w_refs, out_refs, ropes):
            y = jnp.dot(xb, w_ref[...], preferred_element_type=jnp.float32)
            for hb in range(N_HEAD_BLOCKS):
                yh = y[:, hb * LANES:(hb + 1) * LANES]
                if rope:
                    yh = yh * cos_ref[0, rows, :] + _rotate_half(yh, first_half) * sin_ref[0, rows, :]
                out_ref[0, hb, rows, :] = yh.astype(jnp.bfloat16)


def _project(xb, cos, sin, weights, ropes):
    batch, seq, _ = xb.shape
    out_sds = jax.ShapeDtypeStruct((batch, N_HEAD_BLOCKS, seq, LANES), jnp.bfloat16)
    outs = []
    for gi, dilation in enumerate(DILATIONS):
        seg = seq // dilation
        table_spec = pl.BlockSpec((1, seq, LANES), lambda b, gi=gi: (gi, 0, 0))
        out_spec = pl.BlockSpec((1, N_HEAD_BLOCKS, seq, LANES), lambda b: (b, 0, 0, 0))
        outs.append(pl.pallas_call(
            functools.partial(_proj_kernel, ropes=ropes),
            out_shape=[out_sds] * len(weights),
            grid=(batch,),
            in_specs=[pl.BlockSpec((1, seg, dilation * D_MODEL), lambda b: (b, 0, 0)), table_spec, table_spec]
            + [pl.BlockSpec((D_MODEL, ATTN_DIM), lambda b, gi=gi: (0, gi))] * len(weights),
            out_specs=[out_spec] * len(weights),
            compiler_params=pltpu.CompilerParams(dimension_semantics=("arbitrary",),
                                                 vmem_limit_bytes=VMEM_LIMIT_BYTES),
            name="proj_" + "".join("r" if r else "p" for r in ropes) + f"_d{dilation}",
        )(xb.reshape(batch, seg, dilation * D_MODEL), cos, sin, *weights))
    return [list(per_weight) for per_weight in zip(*outs)]


def _attn_kernel(*refs):
    assert HEADS_PER_BLOCK == 2
    q_refs, k_refs, v_refs = refs[:N_GROUPS], refs[N_GROUPS:2 * N_GROUPS], refs[2 * N_GROUPS:3 * N_GROUPS]
    o_ref, acc_ref, m_ref, l_ref = refs[3 * N_GROUPS:]
    seq = o_ref.shape[0]

    def head_masks(rows, dtype):
        head = (lax.broadcasted_iota(jnp.int32, (rows, LANES), 1) // HEAD_DIM).astype(dtype)
        return [head == h for h in range(HEADS_PER_BLOCK)]

    head1_lanes = head_masks(SPAN, jnp.float32)[1]
    q_masks = head_masks(SPAN, jnp.bfloat16)
    v_masks = {rows: head_masks(rows, jnp.bfloat16) for rows in (SPAN, 2 * SPAN)}
    sum_cols = {rows: [mask.astype(jnp.bfloat16) for mask in masks] for rows, masks in v_masks.items()}
    qi = lax.broadcasted_iota(jnp.int32, (HEADS_PER_BLOCK * SPAN, 2 * SPAN), 0) % SPAN
    kj = lax.broadcasted_iota(jnp.int32, (HEADS_PER_BLOCK * SPAN, 2 * SPAN), 1)
    rel = SPAN + qi - kj
    band_bias = jnp.where((rel >= 0) & (rel <= SPAN), 0.0, MASK_VALUE).astype(jnp.float32)
    causal_bias = band_bias[:, SPAN:]

    def key_rows(gi, seg, j):
        blocks_per_seg = seq // DILATIONS[gi] // SPAN
        row0 = (seg * blocks_per_seg + j) * SPAN
        return row0, (pl.ds(row0 - SPAN, 2 * SPAN) if j > 0 else pl.ds(row0, SPAN))

    def scores(gi, seg, j):
        row0, keys = key_rows(gi, seg, j)
        q = q_refs[gi][pl.ds(row0, SPAN), :]
        k = k_refs[gi][keys, :]
        q_heads = jnp.concatenate([jnp.where(mask, q, jnp.zeros_like(q)) for mask in q_masks], axis=0)
        s = lax.dot_general(q_heads, k, (((1,), (1,)), ((), ())), preferred_element_type=jnp.float32)
        return s + (band_bias if j > 0 else causal_bias)

    def finish(gi, seg, j, s):
        row0, keys = key_rows(gi, seg, j)
        v = v_refs[gi][keys, :]
        n_keys = v.shape[0]
        m = jnp.max(s, axis=-1, keepdims=True)
        p = jnp.exp2(s - m).astype(jnp.bfloat16)
        p_heads = jnp.concatenate([p[:SPAN], p[SPAN:]], axis=1)
        rhs = jnp.concatenate(
            [jnp.concatenate([jnp.where(mask, v, jnp.zeros_like(v)), ones], axis=1)
             for mask, ones in zip(v_masks[n_keys], sum_cols[n_keys])], axis=0)
        pv = jnp.dot(p_heads, rhs, preferred_element_type=jnp.float32)
        m_all = jnp.where(head1_lanes, jnp.broadcast_to(m[SPAN:], (SPAN, LANES)),
                          jnp.broadcast_to(m[:SPAN], (SPAN, LANES)))
        dilation = DILATIONS[gi]
        if dilation == 1:
            dst = pl.ds(row0, SPAN)
        else:
            dst = pl.ds(j * SPAN * dilation + seg, SPAN, stride=dilation)
        acc_ref[gi, dst, :] = pv[:, :LANES]
        l_ref[gi, dst, :] = pv[:, LANES:]
        m_ref[gi, dst, :] = m_all

    blocks = [(gi, seg, j)
              for gi, dilation in enumerate(DILATIONS)
              for seg in range(dilation)
              for j in range(seq // dilation // SPAN)]
    pending = [scores(*b) for b in blocks[:ATTN_LOOKAHEAD]]
    for i, b in enumerate(blocks):
        if i + ATTN_LOOKAHEAD < len(blocks):
            pending.append(scores(*blocks[i + ATTN_LOOKAHEAD]))
        finish(*b, pending.pop(0))

    merge_rows = 256
    for c in range(seq // merge_rows):
        rows = pl.ds(c * merge_rows, merge_rows)
        ms = [m_ref[gi, rows, :] for gi in range(N_GROUPS)]
        m_max = functools.reduce(jnp.maximum, ms)
        es = [jnp.exp2(m - m_max) for m in ms]
        num = sum(e * acc_ref[gi, rows, :] for gi, e in enumerate(es))
        den = sum(e * l_ref[gi, rows, :] for gi, e in enumerate(es))
        o_ref[rows, :] = (num / den).astype(jnp.bfloat16)


def _attention(q, k, v):
    batch, _, seq, _ = q[0].shape
    qkv_spec = pl.BlockSpec((None, None, seq, LANES), lambda b, hb: (b, hb, 0, 0))
    return pl.pallas_call(
        _attn_kernel,
        out_shape=jax.ShapeDtypeStruct((batch, N_HEAD_BLOCKS, seq, LANES), jnp.bfloat16),
        grid=(batch, N_HEAD_BLOCKS),
        in_specs=[qkv_spec] * (3 * N_GROUPS),
        out_specs=qkv_spec,
        scratch_shapes=[pltpu.VMEM((N_GROUPS, seq, LANES), jnp.float32)] * 3,
        compiler_params=pltpu.CompilerParams(dimension_semantics=("arbitrary", "arbitrary"),
                                             vmem_limit_bytes=VMEM_LIMIT_BYTES),
        name="dilated_attn",
    )(*q, *k, *v)


def _rope_tables(seq, scale):
    inv_freq = ROPE_THETA ** (-jnp.arange(0, HEAD_DIM, 2, dtype=jnp.float32) / HEAD_DIM)
    lane = jnp.arange(LANES)
    freq = inv_freq[lane % (HEAD_DIM // 2)]
    sign = jnp.where((lane % HEAD_DIM) < HEAD_DIM // 2, -1.0, 1.0).astype(jnp.float32)
    row = jnp.arange(seq)
    cos, sin = [], []
    for dilation in DILATIONS:
        seg = seq // dilation
        pos = ((row % seg) * dilation + row // seg).astype(jnp.float32)
        ang = pos[:, None] * freq[None, :]
        cos.append(jnp.cos(ang) * scale)
        sin.append(jnp.sin(ang) * sign[None, :] * scale)
    return jnp.stack(cos), jnp.stack(sin)


def kernel(x, pool_w, pool_scale, w_q, w_kv, w_o, ffn_w_gate, ffn_w_up, ffn_conv_w, ffn_conv_b,
           ffn_w_down, ln1_g, ln1_b, ln2_g, ln2_b):
    batch, seq, d_model = x.shape
    assert d_model == D_MODEL and seq % (max(DILATIONS) * SPAN) == 0 and seq % FFN_ROWS == 0
    bf16 = jnp.bfloat16
    q_dim = N_GROUPS * ATTN_DIM
    row = lambda a: a.reshape(1, -1)
    cos_q, sin_q = _rope_tables(seq, HEAD_DIM ** -0.5 * LOG2_E)
    cos_k, sin_k = _rope_tables(seq, 1.0)
    k = v = xb = None
    for i in range(DEPTH):
        ffn = (ffn_w_gate[i].astype(bf16), ffn_w_up[i].astype(bf16), ffn_conv_w[i], row(ffn_conv_b[i]),
               ffn_w_down[i].astype(bf16), row(ln2_g[i]), row(ln2_b[i]))
        feeds_projection = N_A_LAYERS - 1 <= i < DEPTH - 1
        if i < N_A_LAYERS:
            outs = _ffn_pool_layer(x, pool_w[i].astype(bf16), row(pool_scale[i]), row(ln1_g[i]), row(ln1_b[i]),
                                   ffn, feeds_projection)
        else:
            j = i - N_A_LAYERS
            (q,) = _project(xb, cos_q, sin_q, [w_q[j].astype(bf16)], (True,))
            o = _attention(q, k, v)
            outs = _ffn_attn_layer(x, o, w_o[j].astype(bf16), row(ln1_g[i]), row(ln1_b[i]), ffn, feeds_projection)
        x, xb = outs if feeds_projection else (outs[0], None)
        if i == N_A_LAYERS - 1:
            k, v = _project(xb, cos_k, sin_k,
                            [w_kv[:, :q_dim].astype(bf16), w_kv[:, q_dim:].astype(bf16)], (True, False))
    return x
```

```python
import functools

import jax
import jax.numpy as jnp
from jax import lax
from jax.experimental import pallas as pl
from jax.experimental.pallas import tpu as pltpu

D_MODEL = 1024
DEPTH = 4
N_A_LAYERS = 2
POOL_WINDOWS = (2, 4, 8, 16)
POOL_GROUP_DIM = D_MODEL // len(POOL_WINDOWS)
POOL_HALO = 16
HEAD_DIM = 64
N_HEADS = 16
DILATED_GROUPS = ((128, 1), (512, 4), (2048, 16))
N_GROUPS = len(DILATED_GROUPS)
DILATIONS = tuple(d for _, d in DILATED_GROUPS)
SPAN = 128
ATTN_DIM = N_HEADS * HEAD_DIM
ROPE_THETA = 10000.0
D_FF = 2816
DEEPNORM_ALPHA = (2.0 * DEPTH) ** 0.25
LN_EPS = 1e-5

LANES = 128
SUBLANES = 8
MXU_DIM = 256
HEADS_PER_BLOCK = LANES // HEAD_DIM
N_HEAD_BLOCKS = ATTN_DIM // LANES
VMEM_LIMIT_BYTES = 56 * 1024 * 1024

FFN_ROWS = 512
FFN_COLS = MXU_DIM
PROJ_ROWS = 512
MASK_VALUE = -1e30
LOG2_E = 1.4426950408889634
ATTN_LOOKAHEAD = 3

assert all(w // d == SPAN for w, d in DILATED_GROUPS)
assert D_FF % FFN_COLS == 0


def _layer_norm(y, gain, bias):
    mu = jnp.mean(y, axis=-1, keepdims=True)
    yc = y - mu
    var = jnp.mean(yc * yc, axis=-1, keepdims=True)
    return yc * lax.rsqrt(var + LN_EPS) * gain + bias


def _resident(shape):
    return pl.BlockSpec(shape, lambda *_: (0,) * len(shape), pipeline_mode=pl.Buffered(1))


def _conv_ffn_ln(x1, first_tile, wg_ref, wu_ref, cw_ref, cb_ref, wd_ref, g2_ref, b2_ref,
                 gcarry_ref, h_ref, out_ref):
    rows = x1.shape[0]
    x1b = x1.astype(jnp.bfloat16)
    for c in range(D_FF // FFN_COLS):
        cols = slice(c * FFN_COLS, (c + 1) * FFN_COLS)
        g = jnp.dot(x1b, wg_ref[:, cols], preferred_element_type=jnp.float32)
        u = jnp.dot(x1b, wu_ref[:, cols], preferred_element_type=jnp.float32)
        carry = jnp.where(first_tile, 0.0, gcarry_ref[:, cols])
        gcarry_ref[:, cols] = g[rows - SUBLANES:, :]
        ge = jnp.concatenate([carry, g], axis=0)
        g1 = pltpu.roll(ge, 1, 0)[SUBLANES:, :]
        g2 = pltpu.roll(ge, 2, 0)[SUBLANES:, :]
        conv = cb_ref[:, cols] + cw_ref[0:1, cols] * g2
        conv = conv + cw_ref[1:2, cols] * g1
        conv = conv + cw_ref[2:3, cols] * g
        h_ref[:, cols] = (jax.nn.gelu(conv, approximate=True) * u).astype(jnp.bfloat16)
    ffn = jnp.dot(h_ref[...], wd_ref[...], preferred_element_type=jnp.float32)
    out_ref[0] = _layer_norm(DEEPNORM_ALPHA * x1 + ffn, g2_ref[...], b2_ref[...])


def _ffn_pool_kernel(x_ref, pw_ref, ps_ref, g1_ref, b1_ref, wg_ref, wu_ref, cw_ref, cb_ref, wd_ref,
                     g2_ref, b2_ref, out_ref, xcarry_ref, gcarry_ref, h_ref):
    tile = pl.program_id(1)
    first_tile = tile == 0
    x = x_ref[0]
    rows = x.shape[0]
    carry = jnp.where(first_tile, 0.0, xcarry_ref[...])
    xcarry_ref[...] = x[rows - POOL_HALO:, :]
    xe = jnp.concatenate([carry, x], axis=0)
    pos = tile * rows + lax.broadcasted_iota(jnp.int32, (rows, 1), 0)
    mixes = []
    for gi, w in enumerate(POOL_WINDOWS):
        cols = slice(gi * POOL_GROUP_DIM, (gi + 1) * POOL_GROUP_DIM)
        s = xe[:, cols]
        k = 1
        while k < w:
            s = s + pltpu.roll(s, k, 0)
            k *= 2
        inv_count = 1.0 / jnp.minimum(pos + 1, w).astype(jnp.float32)
        pooled = s[POOL_HALO:, :] * inv_count
        diff = (pooled - x[:, cols]).astype(jnp.bfloat16)
        mixes.append(jnp.dot(diff, pw_ref[gi], preferred_element_type=jnp.float32))
    mix = jnp.concatenate(mixes, axis=1) * ps_ref[...]
    x1 = _layer_norm(DEEPNORM_ALPHA * x + mix, g1_ref[...], b1_ref[...])
    _conv_ffn_ln(x1, first_tile, wg_ref, wu_ref, cw_ref, cb_ref, wd_ref, g2_ref, b2_ref,
                 gcarry_ref, h_ref, out_ref)


def _ffn_attn_kernel(x_ref, o_ref, wo_ref, g1_ref, b1_ref, wg_ref, wu_ref, cw_ref, cb_ref, wd_ref,
                     g2_ref, b2_ref, out_ref, gcarry_ref, h_ref):
    first_tile = pl.program_id(1) == 0
    o = jnp.concatenate([o_ref[0, hb] for hb in range(N_HEAD_BLOCKS)], axis=1)
    mix = jnp.dot(o, wo_ref[...], preferred_element_type=jnp.float32)
    x1 = _layer_norm(DEEPNORM_ALPHA * x_ref[0] + mix, g1_ref[...], b1_ref[...])
    _conv_ffn_ln(x1, first_tile, wg_ref, wu_ref, cw_ref, cb_ref, wd_ref, g2_ref, b2_ref,
                 gcarry_ref, h_ref, out_ref)


def _ffn_specs():
    return [_resident((D_MODEL, D_FF)), _resident((D_MODEL, D_FF)), _resident((3, D_FF)),
            _resident((1, D_FF)), _resident((D_FF, D_MODEL)), _resident((1, D_MODEL)),
            _resident((1, D_MODEL))]


def _ffn_scratch():
    return [pltpu.VMEM((SUBLANES, D_FF), jnp.float32), pltpu.VMEM((FFN_ROWS, D_FF), jnp.bfloat16)]


def _row_tile_spec():
    return pl.BlockSpec((1, FFN_ROWS, D_MODEL), lambda b, t: (b, t, 0))


def _ffn_params():
    return pltpu.CompilerParams(dimension_semantics=("arbitrary", "arbitrary"),
                                vmem_limit_bytes=VMEM_LIMIT_BYTES)


def _ffn_pool_layer(x, pw, ps, g1, b1, ffn):
    batch, seq, _ = x.shape
    return pl.pallas_call(
        _ffn_pool_kernel,
        out_shape=jax.ShapeDtypeStruct(x.shape, jnp.float32),
        grid=(batch, seq // FFN_ROWS),
        in_specs=[_row_tile_spec(),
                  _resident((len(POOL_WINDOWS), POOL_GROUP_DIM, POOL_GROUP_DIM)),
                  _resident((1, D_MODEL)), _resident((1, D_MODEL)), _resident((1, D_MODEL))]
        + _ffn_specs(),
        out_specs=_row_tile_spec(),
        scratch_shapes=[pltpu.VMEM((POOL_HALO, D_MODEL), jnp.float32)] + _ffn_scratch(),
        compiler_params=_ffn_params(),
        name="ffn_pool",
    )(x, pw, ps, g1, b1, *ffn)


def _ffn_attn_layer(x, o, wo, g1, b1, ffn):
    batch, seq, _ = x.shape
    o_spec = pl.BlockSpec((1, N_HEAD_BLOCKS, FFN_ROWS, LANES), lambda b, t: (b, 0, t, 0))
    return pl.pallas_call(
        _ffn_attn_kernel,
        out_shape=jax.ShapeDtypeStruct(x.shape, jnp.float32),
        grid=(batch, seq // FFN_ROWS),
        in_specs=[_row_tile_spec(), o_spec, _resident((ATTN_DIM, D_MODEL)),
                  _resident((1, D_MODEL)), _resident((1, D_MODEL))] + _ffn_specs(),
        out_specs=_row_tile_spec(),
        scratch_shapes=_ffn_scratch(),
        compiler_params=_ffn_params(),
        name="ffn_attn",
    )(x, o, wo, g1, b1, *ffn)


def _dilated_rows(xcol_refs, first_row, n_rows, dilation):
    seq = xcol_refs[0].shape[1]
    seg = seq // dilation
    pieces = []
    done = 0
    while done < n_rows:
        residue, m0 = divmod(first_row + done, seg)
        n = min(n_rows - done, seg - m0)
        rows = pl.ds(m0 * dilation + residue, n, stride=dilation) if dilation > 1 else pl.ds(m0, n)
        pieces.append(jnp.concatenate([ref[0, rows, :] for ref in xcol_refs], axis=1))
        done += n
    return pieces[0] if len(pieces) == 1 else jnp.concatenate(pieces, axis=0)


def _rotate_half(t, first_half):
    return jnp.where(first_half, pltpu.roll(t, LANES - HEAD_DIM // 2, 1), pltpu.roll(t, HEAD_DIM // 2, 1))


def _proj_kernel(*refs, ropes):
    n_cols, n_out = D_MODEL // LANES, len(ropes)
    xcol_refs, (cos_ref, sin_ref) = refs[:n_cols], refs[n_cols:n_cols + 2]
    w_refs, out_refs = refs[n_cols + 2:n_cols + 2 + n_out], refs[n_cols + 2 + n_out:]
    seq = cos_ref.shape[1]
    group = pl.program_id(1)
    lane = lax.broadcasted_iota(jnp.int32, (PROJ_ROWS, LANES), 1)
    first_half = (lane % HEAD_DIM) < HEAD_DIM // 2
    for gi, dilation in enumerate(DILATIONS):
        @pl.when(group == gi)
        def _(dilation=dilation):
            for c in range(seq // PROJ_ROWS):
                rows = pl.ds(c * PROJ_ROWS, PROJ_ROWS)
                xb = _dilated_rows(xcol_refs, c * PROJ_ROWS, PROJ_ROWS, dilation).astype(jnp.bfloat16)
                for w_ref, out_ref, rope in zip(w_refs, out_refs, ropes):
                    y = jnp.dot(xb, w_ref[...], preferred_element_type=jnp.float32)
                    for hb in range(N_HEAD_BLOCKS):
                        yh = y[:, hb * LANES:(hb + 1) * LANES]
                        if rope:
                            yh = yh * cos_ref[0, rows, :] + _rotate_half(yh, first_half) * sin_ref[0, rows, :]
                        out_ref[0, 0, hb, rows, :] = yh.astype(jnp.bfloat16)


def _project(x, cos, sin, weights, ropes):
    batch, seq, _ = x.shape
    out_sds = jax.ShapeDtypeStruct((batch, N_GROUPS, N_HEAD_BLOCKS, seq, LANES), jnp.bfloat16)
    table_spec = pl.BlockSpec((1, seq, LANES), lambda b, g: (g, 0, 0))
    n_cols = D_MODEL // LANES
    return pl.pallas_call(
        functools.partial(_proj_kernel, ropes=ropes),
        out_shape=[out_sds] * len(weights),
        grid=(batch, N_GROUPS),
        in_specs=[pl.BlockSpec((1, seq, LANES), lambda b, g, cb=cb: (b, 0, cb)) for cb in range(n_cols)]
        + [table_spec, table_spec]
        + [pl.BlockSpec((D_MODEL, ATTN_DIM), lambda b, g: (0, g))] * len(weights),
        out_specs=[pl.BlockSpec((1, 1, N_HEAD_BLOCKS, seq, LANES), lambda b, g: (b, g, 0, 0, 0))] * len(weights),
        compiler_params=pltpu.CompilerParams(dimension_semantics=("arbitrary", "arbitrary"),
                                             vmem_limit_bytes=VMEM_LIMIT_BYTES),
        name="proj_" + "".join("r" if r else "p" for r in ropes),
    )(*[x] * n_cols, cos, sin, *weights)


def _attn_kernel(*refs):
    assert HEADS_PER_BLOCK == 2
    q_refs, k_refs, v_refs = refs[:N_GROUPS], refs[N_GROUPS:2 * N_GROUPS], refs[2 * N_GROUPS:3 * N_GROUPS]
    o_ref, part_ref, lse_ref = refs[3 * N_GROUPS:]
    seq = o_ref.shape[0]

    def head_masks(rows, dtype):
        head = (lax.broadcasted_iota(jnp.int32, (rows, LANES), 1) // HEAD_DIM).astype(dtype)
        return [head == h for h in range(HEADS_PER_BLOCK)]

    head1_lanes = head_masks(SPAN, jnp.float32)[1]
    q_masks = head_masks(SPAN, jnp.bfloat16)
    v_masks = {rows: head_masks(rows, jnp.bfloat16) for rows in (SPAN, 2 * SPAN)}
    sum_cols = {rows: [mask.astype(jnp.bfloat16) for mask in masks] for rows, masks in v_masks.items()}
    qi = lax.broadcasted_iota(jnp.int32, (HEADS_PER_BLOCK * SPAN, 2 * SPAN), 0) % SPAN
    kj = lax.broadcasted_iota(jnp.int32, (HEADS_PER_BLOCK * SPAN, 2 * SPAN), 1)
    rel = SPAN + qi - kj
    band_bias = jnp.where((rel >= 0) & (rel <= SPAN), 0.0, MASK_VALUE).astype(jnp.float32)
    causal_bias = band_bias[:, SPAN:]

    def key_rows(gi, seg, j):
        blocks_per_seg = seq // DILATIONS[gi] // SPAN
        row0 = (seg * blocks_per_seg + j) * SPAN
        return row0, (pl.ds(row0 - SPAN, 2 * SPAN) if j > 0 else pl.ds(row0, SPAN))

    def scores(gi, seg, j):
        row0, keys = key_rows(gi, seg, j)
        q = q_refs[gi][pl.ds(row0, SPAN), :]
        k = k_refs[gi][keys, :]
        q_heads = jnp.concatenate([jnp.where(mask, q, jnp.zeros_like(q)) for mask in q_masks], axis=0)
        s = lax.dot_general(q_heads, k, (((1,), (1,)), ((), ())), preferred_element_type=jnp.float32)
        return s + (band_bias if j > 0 else causal_bias)

    def finish(gi, seg, j, s):
        row0, keys = key_rows(gi, seg, j)
        v = v_refs[gi][keys, :]
        n_keys = v.shape[0]
        m = jnp.max(s, axis=-1, keepdims=True)
        p = jnp.exp2(s - m).astype(jnp.bfloat16)
        p_heads = jnp.concatenate([p[:SPAN], p[SPAN:]], axis=1)
        rhs = jnp.concatenate(
            [jnp.concatenate([jnp.where(mask, v, jnp.zeros_like(v)), ones], axis=1)
             for mask, ones in zip(v_masks[n_keys], sum_cols[n_keys])], axis=0)
        pv = jnp.dot(p_heads, rhs, preferred_element_type=jnp.float32)
        m_all = jnp.where(head1_lanes, jnp.broadcast_to(m[SPAN:], (SPAN, LANES)),
                          jnp.broadcast_to(m[:SPAN], (SPAN, LANES)))
        l = pv[:, LANES:]
        out = pv[:, :LANES] / l
        lse = m_all + jnp.log2(l)
        dilation = DILATIONS[gi]
        if dilation > 1:
            dst = pl.ds(j * SPAN * dilation + seg, SPAN, stride=dilation)
            part_ref[gi - 1, dst, :] = out
            lse_ref[gi - 1, dst, :] = lse
        else:
            rows = pl.ds(row0, SPAN)
            outs = [out] + [part_ref[g, rows, :] for g in range(N_GROUPS - 1)]
            lses = [lse] + [lse_ref[g, rows, :] for g in range(N_GROUPS - 1)]
            lse_max = functools.reduce(jnp.maximum, lses)
            es = [jnp.exp2(x - lse_max) for x in lses]
            num = sum(e * o for e, o in zip(es, outs))
            o_ref[rows, :] = (num / sum(es)).astype(jnp.bfloat16)

    blocks = [(gi, seg, j)
              for gi, dilation in sorted(enumerate(DILATIONS), key=lambda e: -e[1])
              for seg in range(dilation)
              for j in range(seq // dilation // SPAN)]
    assert DILATIONS[0] == 1 and blocks[-1][0] == 0
    pending = [scores(*b) for b in blocks[:ATTN_LOOKAHEAD]]
    for i, b in enumerate(blocks):
        if i + ATTN_LOOKAHEAD < len(blocks):
            pending.append(scores(*blocks[i + ATTN_LOOKAHEAD]))
        finish(*b, pending.pop(0))


def _attention(q, k, v):
    batch, _, _, seq, _ = q.shape
    group_specs = [pl.BlockSpec((None, None, None, seq, LANES), lambda b, hb, gi=gi: (b, gi, hb, 0, 0))
                   for gi in range(N_GROUPS)]
    return pl.pallas_call(
        _attn_kernel,
        out_shape=jax.ShapeDtypeStruct((batch, N_HEAD_BLOCKS, seq, LANES), jnp.bfloat16),
        grid=(batch, N_HEAD_BLOCKS),
        in_specs=group_specs * 3,
        out_specs=pl.BlockSpec((None, None, seq, LANES), lambda b, hb: (b, hb, 0, 0)),
        scratch_shapes=[pltpu.VMEM((N_GROUPS - 1, seq, LANES), jnp.float32)] * 2,
        compiler_params=pltpu.CompilerParams(dimension_semantics=("arbitrary", "arbitrary"),
                                             vmem_limit_bytes=VMEM_LIMIT_BYTES),
        name="dilated_attn",
    )(*[q] * N_GROUPS, *[k] * N_GROUPS, *[v] * N_GROUPS)


def _rope_tables(seq, scale):
    inv_freq = ROPE_THETA ** (-jnp.arange(0, HEAD_DIM, 2, dtype=jnp.float32) / HEAD_DIM)
    lane = jnp.arange(LANES)
    freq = inv_freq[lane % (HEAD_DIM // 2)]
    sign = jnp.where((lane % HEAD_DIM) < HEAD_DIM // 2, -1.0, 1.0).astype(jnp.float32)
    row = jnp.arange(seq)
    cos, sin = [], []
    for dilation in DILATIONS:
        seg = seq // dilation
        pos = ((row % seg) * dilation + row // seg).astype(jnp.float32)
        ang = pos[:, None] * freq[None, :]
        cos.append(jnp.cos(ang) * scale)
        sin.append(jnp.sin(ang) * sign[None, :] * scale)
    return jnp.stack(cos), jnp.stack(sin)


def kernel(x, pool_w, pool_scale, w_q, w_kv, w_o, ffn_w_gate, ffn_w_up, ffn_conv_w, ffn_conv_b,
           ffn_w_down, ln1_g, ln1_b, ln2_g, ln2_b):
    batch, seq, d_model = x.shape
    assert d_model == D_MODEL and seq % (max(DILATIONS) * SPAN) == 0 and seq % FFN_ROWS == 0
    bf16 = jnp.bfloat16
    q_dim = N_GROUPS * ATTN_DIM
    row = lambda a: a.reshape(1, -1)
    cos_q, sin_q = _rope_tables(seq, HEAD_DIM ** -0.5 * LOG2_E)
    cos_k, sin_k = _rope_tables(seq, 1.0)
    k = v = None
    for i in range(DEPTH):
        ffn = (ffn_w_gate[i].astype(bf16), ffn_w_up[i].astype(bf16), ffn_conv_w[i], row(ffn_conv_b[i]),
               ffn_w_down[i].astype(bf16), row(ln2_g[i]), row(ln2_b[i]))
        if i < N_A_LAYERS:
            x = _ffn_pool_layer(x, pool_w[i].astype(bf16), row(pool_scale[i]), row(ln1_g[i]), row(ln1_b[i]), ffn)
        else:
            j = i - N_A_LAYERS
            (q,) = _project(x, cos_q, sin_q, [w_q[j].astype(bf16)], (True,))
            o = _attention(q, k, v)
            x = _ffn_attn_layer(x, o, w_o[j].astype(bf16), row(ln1_g[i]), row(ln1_b[i]), ffn)
        if i == N_A_LAYERS - 1:
            k, v = _project(x, cos_k, sin_k,
                            [w_kv[:, :q_dim].astype(bf16), w_kv[:, q_dim:].astype(bf16)], (True, False))
    return x
```

```python
import functools

import jax
import jax.numpy as jnp
from jax import lax
from jax.experimental import pallas as pl
from jax.experimental.pallas import tpu as pltpu

D_MODEL = 1024
DEPTH = 4
N_A_LAYERS = 2
POOL_WINDOWS = (2, 4, 8, 16)
POOL_GROUP_DIM = D_MODEL // len(POOL_WINDOWS)
POOL_HALO = 16
HEAD_DIM = 64
N_HEADS = 16
DILATED_GROUPS = ((128, 1), (512, 4), (2048, 16))
N_GROUPS = len(DILATED_GROUPS)
DILATIONS = tuple(d for _, d in DILATED_GROUPS)
SPAN = 128
ATTN_DIM = N_HEADS * HEAD_DIM
ROPE_THETA = 10000.0
D_FF = 2816
DEEPNORM_ALPHA = (2.0 * DEPTH) ** 0.25
LN_EPS = 1e-5

LANES = 128
SUBLANES = 8
MXU_DIM = 256
HEADS_PER_BLOCK = LANES // HEAD_DIM
N_HEAD_BLOCKS = ATTN_DIM // LANES
VMEM_LIMIT_BYTES = 56 * 1024 * 1024

FFN_ROWS = 512
FFN_TILES_PER_STEP = 2
FFN_COLS = MXU_DIM
FFN_DOWN_ROWS = 256
PROJ_ROWS = 256
MASK_VALUE = -1e30
LOG2_E = 1.4426950408889634
ATTN_HEAD_BLOCKS = 4
ATTN_LOOKAHEAD = 3

assert all(w // d == SPAN for w, d in DILATED_GROUPS)
assert D_FF % FFN_COLS == 0 and FFN_ROWS % FFN_DOWN_ROWS == 0


def _layer_norm(y, gain, bias):
    mu = jnp.mean(y, axis=-1, keepdims=True)
    yc = y - mu
    var = jnp.mean(yc * yc, axis=-1, keepdims=True)
    return yc * lax.rsqrt(var + LN_EPS) * gain + bias


def _resident(shape):
    return pl.BlockSpec(shape, lambda *_: (0,) * len(shape), pipeline_mode=pl.Buffered(1))


def _conv_ffn_ln(x1, first_tile, wg_ref, wu_ref, cw_ref, cb_ref, wd_ref, g2_ref, b2_ref, gcarry_ref, h_ref):
    rows = x1.shape[0]
    x1b = x1.astype(jnp.bfloat16)
    for c in range(D_FF // FFN_COLS):
        cols = slice(c * FFN_COLS, (c + 1) * FFN_COLS)
        g = jnp.dot(x1b, wg_ref[:, cols], preferred_element_type=jnp.float32)
        u = jnp.dot(x1b, wu_ref[:, cols], preferred_element_type=jnp.float32)
        carry = jnp.where(first_tile, 0.0, gcarry_ref[:, cols])
        gcarry_ref[:, cols] = g[rows - SUBLANES:, :]
        ge = jnp.concatenate([carry, g], axis=0)
        g1 = pltpu.roll(ge, 1, 0)[SUBLANES:, :]
        g2 = pltpu.roll(ge, 2, 0)[SUBLANES:, :]
        conv = cb_ref[:, cols] + cw_ref[0:1, cols] * g2
        conv = conv + cw_ref[1:2, cols] * g1
        conv = conv + cw_ref[2:3, cols] * g
        h_ref[:, cols] = (jax.nn.gelu(conv, approximate=True) * u).astype(jnp.bfloat16)
    outs = []
    for r in range(0, rows, FFN_DOWN_ROWS):
        ffn = jnp.dot(h_ref[r:r + FFN_DOWN_ROWS, :], wd_ref[...], preferred_element_type=jnp.float32)
        outs.append(_layer_norm(DEEPNORM_ALPHA * x1[r:r + FFN_DOWN_ROWS] + ffn, g2_ref[...], b2_ref[...]))
    return jnp.concatenate(outs, axis=0)


def _ffn_pool_kernel(x_ref, pw_ref, ps_ref, g1_ref, b1_ref, wg_ref, wu_ref, cw_ref, cb_ref, wd_ref,
                     g2_ref, b2_ref, out_ref, xcarry_ref, gcarry_ref, h_ref):
    def tile_body(t, carry):
        tile = pl.program_id(1) * FFN_TILES_PER_STEP + t
        first_tile = tile == 0
        rows = pl.ds(pl.multiple_of(t * FFN_ROWS, FFN_ROWS), FFN_ROWS)
        x = x_ref[0, rows, :]
        halo = jnp.where(first_tile, 0.0, xcarry_ref[...])
        xcarry_ref[...] = x[FFN_ROWS - POOL_HALO:, :]
        xe = jnp.concatenate([halo, x], axis=0)
        pos = tile * FFN_ROWS + lax.broadcasted_iota(jnp.int32, (FFN_ROWS, 1), 0)
        mixes = []
        for gi, w in enumerate(POOL_WINDOWS):
            cols = slice(gi * POOL_GROUP_DIM, (gi + 1) * POOL_GROUP_DIM)
            s = xe[:, cols]
            k = 1
            while k < w:
                s = s + pltpu.roll(s, k, 0)
                k *= 2
            inv_count = 1.0 / jnp.minimum(pos + 1, w).astype(jnp.float32)
            pooled = s[POOL_HALO:, :] * inv_count
            diff = (pooled - x[:, cols]).astype(jnp.bfloat16)
            mixes.append(jnp.dot(diff, pw_ref[gi], preferred_element_type=jnp.float32))
        mix = jnp.concatenate(mixes, axis=1) * ps_ref[...]
        x1 = _layer_norm(DEEPNORM_ALPHA * x + mix, g1_ref[...], b1_ref[...])
        out_ref[0, rows, :] = _conv_ffn_ln(x1, first_tile, wg_ref, wu_ref, cw_ref, cb_ref, wd_ref, g2_ref, b2_ref,
                                           gcarry_ref, h_ref)
        return carry

    lax.fori_loop(0, FFN_TILES_PER_STEP, tile_body, 0)


def _ffn_attn_kernel(x_ref, o_ref, wo_ref, g1_ref, b1_ref, wg_ref, wu_ref, cw_ref, cb_ref, wd_ref,
                     g2_ref, b2_ref, out_ref, gcarry_ref, h_ref):
    def tile_body(t, carry):
        first_tile = pl.program_id(1) * FFN_TILES_PER_STEP + t == 0
        rows = pl.ds(pl.multiple_of(t * FFN_ROWS, FFN_ROWS), FFN_ROWS)
        o = jnp.concatenate([o_ref[0, hb, rows, :] for hb in range(N_HEAD_BLOCKS)], axis=1)
        mix = jnp.dot(o, wo_ref[...], preferred_element_type=jnp.float32)
        x1 = _layer_norm(DEEPNORM_ALPHA * x_ref[0, rows, :] + mix, g1_ref[...], b1_ref[...])
        out_ref[0, rows, :] = _conv_ffn_ln(x1, first_tile, wg_ref, wu_ref, cw_ref, cb_ref, wd_ref, g2_ref, b2_ref,
                                           gcarry_ref, h_ref)
        return carry

    lax.fori_loop(0, FFN_TILES_PER_STEP, tile_body, 0)


def _ffn_specs():
    return [_resident((D_MODEL, D_FF)), _resident((D_MODEL, D_FF)), _resident((3, D_FF)),
            _resident((1, D_FF)), _resident((D_FF, D_MODEL)), _resident((1, D_MODEL)),
            _resident((1, D_MODEL))]


def _ffn_scratch():
    return [pltpu.VMEM((SUBLANES, D_FF), jnp.float32), pltpu.VMEM((FFN_ROWS, D_FF), jnp.bfloat16)]


def _row_tile_spec():
    return pl.BlockSpec((1, FFN_TILES_PER_STEP * FFN_ROWS, D_MODEL), lambda b, t: (b, t, 0))


def _ffn_params():
    return pltpu.CompilerParams(dimension_semantics=("arbitrary", "arbitrary"),
                                vmem_limit_bytes=VMEM_LIMIT_BYTES)


def _ffn_pool_layer(x, pw, ps, g1, b1, ffn):
    batch, seq, _ = x.shape
    return pl.pallas_call(
        _ffn_pool_kernel,
        out_shape=jax.ShapeDtypeStruct(x.shape, jnp.float32),
        grid=(batch, seq // (FFN_TILES_PER_STEP * FFN_ROWS)),
        in_specs=[_row_tile_spec(),
                  _resident((len(POOL_WINDOWS), POOL_GROUP_DIM, POOL_GROUP_DIM)),
                  _resident((1, D_MODEL)), _resident((1, D_MODEL)), _resident((1, D_MODEL))]
        + _ffn_specs(),
        out_specs=_row_tile_spec(),
        scratch_shapes=[pltpu.VMEM((POOL_HALO, D_MODEL), jnp.float32)] + _ffn_scratch(),
        compiler_params=_ffn_params(),
        name="ffn_pool",
    )(x, pw, ps, g1, b1, *ffn)


def _ffn_attn_layer(x, o, wo, g1, b1, ffn):
    batch, seq, _ = x.shape
    o_spec = pl.BlockSpec((1, N_HEAD_BLOCKS, FFN_TILES_PER_STEP * FFN_ROWS, LANES), lambda b, t: (b, 0, t, 0))
    return pl.pallas_call(
        _ffn_attn_kernel,
        out_shape=jax.ShapeDtypeStruct(x.shape, jnp.float32),
        grid=(batch, seq // (FFN_TILES_PER_STEP * FFN_ROWS)),
        in_specs=[_row_tile_spec(), o_spec, _resident((ATTN_DIM, D_MODEL)),
                  _resident((1, D_MODEL)), _resident((1, D_MODEL))] + _ffn_specs(),
        out_specs=_row_tile_spec(),
        scratch_shapes=_ffn_scratch(),
        compiler_params=_ffn_params(),
        name="ffn_attn",
    )(x, o, wo, g1, b1, *ffn)


def _dilated_rows(xcol_refs, first_row, n_rows, dilation):
    seq = xcol_refs[0].shape[1]
    seg = seq // dilation
    pieces = []
    done = 0
    while done < n_rows:
        residue, m0 = divmod(first_row + done, seg)
        n = min(n_rows - done, seg - m0)
        rows = pl.ds(m0 * dilation + residue, n, stride=dilation) if dilation > 1 else pl.ds(m0, n)
        pieces.append(jnp.concatenate([ref[0, rows, :] for ref in xcol_refs], axis=1))
        done += n
    return pieces[0] if len(pieces) == 1 else jnp.concatenate(pieces, axis=0)


def _rotate_half(t, first_half):
    return jnp.where(first_half, pltpu.roll(t, LANES - HEAD_DIM // 2, 1), pltpu.roll(t, HEAD_DIM // 2, 1))


def _proj_kernel(*refs, ropes):
    n_cols, n_out = D_MODEL // LANES, len(ropes)
    xcol_refs, (cos_ref, sin_ref) = refs[:n_cols], refs[n_cols:n_cols + 2]
    w_refs, out_refs = refs[n_cols + 2:n_cols + 2 + n_out], refs[n_cols + 2 + n_out:]
    seq = cos_ref.shape[1]
    group = pl.program_id(1)
    lane = lax.broadcasted_iota(jnp.int32, (PROJ_ROWS, LANES), 1)
    first_half = (lane % HEAD_DIM) < HEAD_DIM // 2
    for gi, dilation in enumerate(DILATIONS):
        @pl.when(group == gi)
        def _(dilation=dilation):
            for c in range(seq // PROJ_ROWS):
                rows = pl.ds(c * PROJ_ROWS, PROJ_ROWS)
                xb = _dilated_rows(xcol_refs, c * PROJ_ROWS, PROJ_ROWS, dilation).astype(jnp.bfloat16)
                for w_ref, out_ref, rope in zip(w_refs, out_refs, ropes):
                    y = jnp.dot(xb, w_ref[...], preferred_element_type=jnp.float32)
                    for hb in range(N_HEAD_BLOCKS):
                        yh = y[:, hb * LANES:(hb + 1) * LANES]
                        if rope:
                            yh = yh * cos_ref[0, rows, :] + _rotate_half(yh, first_half) * sin_ref[0, rows, :]
                        out_ref[0, 0, hb, rows, :] = yh.astype(jnp.bfloat16)


def _project(x, cos, sin, weights, ropes):
    batch, seq, _ = x.shape
    out_sds = jax.ShapeDtypeStruct((batch, N_GROUPS, N_HEAD_BLOCKS, seq, LANES), jnp.bfloat16)
    table_spec = pl.BlockSpec((1, seq, LANES), lambda b, g: (g, 0, 0))
    n_cols = D_MODEL // LANES
    return pl.pallas_call(
        functools.partial(_proj_kernel, ropes=ropes),
        out_shape=[out_sds] * len(weights),
        grid=(batch, N_GROUPS),
        in_specs=[pl.BlockSpec((1, seq, LANES), lambda b, g, cb=cb: (b, 0, cb)) for cb in range(n_cols)]
        + [table_spec, table_spec]
        + [pl.BlockSpec((D_MODEL, ATTN_DIM), lambda b, g: (0, g))] * len(weights),
        out_specs=[pl.BlockSpec((1, 1, N_HEAD_BLOCKS, seq, LANES), lambda b, g: (b, g, 0, 0, 0))] * len(weights),
        compiler_params=pltpu.CompilerParams(dimension_semantics=("arbitrary", "arbitrary"),
                                             vmem_limit_bytes=VMEM_LIMIT_BYTES),
        name="proj_" + "".join("r" if r else "p" for r in ropes),
    )(*[x] * n_cols, cos, sin, *weights)


def _attn_kernel(*refs):
    assert HEADS_PER_BLOCK == 2
    q_refs, k_refs, v_refs = refs[:N_GROUPS], refs[N_GROUPS:2 * N_GROUPS], refs[2 * N_GROUPS:3 * N_GROUPS]
    o_ref, part_ref, lse_ref = refs[3 * N_GROUPS:]
    seq = o_ref.shape[1]

    def head_masks(rows, dtype):
        head = (lax.broadcasted_iota(jnp.int32, (rows, LANES), 1) // HEAD_DIM).astype(dtype)
        return [head == h for h in range(HEADS_PER_BLOCK)]

    head1_lanes = head_masks(SPAN, jnp.float32)[1]
    q_masks = head_masks(SPAN, jnp.bfloat16)
    v_masks = {rows: head_masks(rows, jnp.bfloat16) for rows in (SPAN, 2 * SPAN)}
    sum_cols = {rows: [mask.astype(jnp.bfloat16) for mask in masks] for rows, masks in v_masks.items()}
    qi = lax.broadcasted_iota(jnp.int32, (HEADS_PER_BLOCK * SPAN, 2 * SPAN), 0) % SPAN
    kj = lax.broadcasted_iota(jnp.int32, (HEADS_PER_BLOCK * SPAN, 2 * SPAN), 1)
    rel = SPAN + qi - kj
    band_bias = jnp.where((rel >= 0) & (rel <= SPAN), 0.0, MASK_VALUE).astype(jnp.float32)
    causal_bias = band_bias[:, SPAN:]

    def key_rows(gi, seg, j):
        blocks_per_seg = seq // DILATIONS[gi] // SPAN
        row0 = (seg * blocks_per_seg + j) * SPAN
        return row0, (pl.ds(row0 - SPAN, 2 * SPAN) if j > 0 else pl.ds(row0, SPAN))

    def scores(hb, gi, seg, j):
        row0, keys = key_rows(gi, seg, j)
        q = q_refs[gi][hb, pl.ds(row0, SPAN), :]
        k = k_refs[gi][hb, keys, :]
        q_heads = jnp.concatenate([jnp.where(mask, q, jnp.zeros_like(q)) for mask in q_masks], axis=0)
        s = lax.dot_general(q_heads, k, (((1,), (1,)), ((), ())), preferred_element_type=jnp.float32)
        return s + (band_bias if j > 0 else causal_bias)

    def finish(hb, gi, seg, j, s):
        row0, keys = key_rows(gi, seg, j)
        v = v_refs[gi][hb, keys, :]
        n_keys = v.shape[0]
        m = jnp.max(s, axis=-1, keepdims=True)
        p = jnp.exp2(s - m).astype(jnp.bfloat16)
        p_heads = jnp.concatenate([p[:SPAN], p[SPAN:]], axis=1)
        rhs = jnp.concatenate(
            [jnp.concatenate([jnp.where(mask, v, jnp.zeros_like(v)), ones], axis=1)
             for mask, ones in zip(v_masks[n_keys], sum_cols[n_keys])], axis=0)
        pv = jnp.dot(p_heads, rhs, preferred_element_type=jnp.float32)
        m_all = jnp.where(head1_lanes, jnp.broadcast_to(m[SPAN:], (SPAN, LANES)),
                          jnp.broadcast_to(m[:SPAN], (SPAN, LANES)))
        l = pv[:, LANES:]
        out = pv[:, :LANES] / l
        lse = m_all + jnp.log2(l)
        dilation = DILATIONS[gi]
        if dilation > 1:
            dst = pl.ds(j * SPAN * dilation + seg, SPAN, stride=dilation)
            part_ref[gi - 1, dst, :] = out
            lse_ref[gi - 1, dst, :] = lse
        else:
            rows = pl.ds(row0, SPAN)
            outs = [out] + [part_ref[g, rows, :] for g in range(N_GROUPS - 1)]
            lses = [lse] + [lse_ref[g, rows, :] for g in range(N_GROUPS - 1)]
            lse_max = functools.reduce(jnp.maximum, lses)
            es = [jnp.exp2(x - lse_max) for x in lses]
            num = sum(e * o for e, o in zip(es, outs))
            o_ref[hb, rows, :] = (num / sum(es)).astype(jnp.bfloat16)

    blocks = [(gi, seg, j)
              for gi, dilation in sorted(enumerate(DILATIONS), key=lambda e: -e[1])
              for seg in range(dilation)
              for j in range(seq // dilation // SPAN)]
    assert DILATIONS[0] == 1 and blocks[-1][0] == 0
    def head_block(hb, carry):
        pending = [scores(hb, *b) for b in blocks[:ATTN_LOOKAHEAD]]
        for i, b in enumerate(blocks):
            if i + ATTN_LOOKAHEAD < len(blocks):
                pending.append(scores(hb, *blocks[i + ATTN_LOOKAHEAD]))
            finish(hb, *b, pending.pop(0))
        return carry

    lax.fori_loop(0, o_ref.shape[0], head_block, 0)


def _attention(q, k, v):
    batch, _, _, seq, _ = q.shape
    group_specs = [pl.BlockSpec((None, None, ATTN_HEAD_BLOCKS, seq, LANES), lambda b, hb, gi=gi: (b, gi, hb, 0, 0))
                   for gi in range(N_GROUPS)]
    return pl.pallas_call(
        _attn_kernel,
        out_shape=jax.ShapeDtypeStruct((batch, N_HEAD_BLOCKS, seq, LANES), jnp.bfloat16),
        grid=(batch, N_HEAD_BLOCKS // ATTN_HEAD_BLOCKS),
        in_specs=group_specs * 3,
        out_specs=pl.BlockSpec((None, ATTN_HEAD_BLOCKS, seq, LANES), lambda b, hb: (b, hb, 0, 0)),
        scratch_shapes=[pltpu.VMEM((N_GROUPS - 1, seq, LANES), jnp.float32)] * 2,
        compiler_params=pltpu.CompilerParams(dimension_semantics=("arbitrary", "arbitrary"),
                                             vmem_limit_bytes=VMEM_LIMIT_BYTES),
        name="dilated_attn",
    )(*[q] * N_GROUPS, *[k] * N_GROUPS, *[v] * N_GROUPS)


def _rope_tables(seq, scale):
    inv_freq = ROPE_THETA ** (-jnp.arange(0, HEAD_DIM, 2, dtype=jnp.float32) / HEAD_DIM)
    lane = jnp.arange(LANES)
    freq = inv_freq[lane % (HEAD_DIM // 2)]
    sign = jnp.where((lane % HEAD_DIM) < HEAD_DIM // 2, -1.0, 1.0).astype(jnp.float32)
    row = jnp.arange(seq)
    cos, sin = [], []
    for dilation in DILATIONS:
        seg = seq // dilation
        pos = ((row % seg) * dilation + row // seg).astype(jnp.float32)
        ang = pos[:, None] * freq[None, :]
        cos.append(jnp.cos(ang) * scale)
        sin.append(jnp.sin(ang) * sign[None, :] * scale)
    return jnp.stack(cos), jnp.stack(sin)


def kernel(x, pool_w, pool_scale, w_q, w_kv, w_o, ffn_w_gate, ffn_w_up, ffn_conv_w, ffn_conv_b,
           ffn_w_down, ln1_g, ln1_b, ln2_g, ln2_b):
    batch, seq, d_model = x.shape
    assert d_model == D_MODEL and seq % (max(DILATIONS) * SPAN) == 0 and seq % (FFN_TILES_PER_STEP * FFN_ROWS) == 0
    bf16 = jnp.bfloat16
    q_dim = N_GROUPS * ATTN_DIM
    row = lambda a: a.reshape(1, -1)
    cos_q, sin_q = _rope_tables(seq, HEAD_DIM ** -0.5 * LOG2_E)
    cos_k, sin_k = _rope_tables(seq, 1.0)
    k = v = None
    for i in range(DEPTH):
        ffn = (ffn_w_gate[i].astype(bf16), ffn_w_up[i].astype(bf16), ffn_conv_w[i], row(ffn_conv_b[i]),
               ffn_w_down[i].astype(bf16), row(ln2_g[i]), row(ln2_b[i]))
        if i < N_A_LAYERS:
            x = _ffn_pool_layer(x, pool_w[i].astype(bf16), row(pool_scale[i]), row(ln1_g[i]), row(ln1_b[i]), ffn)
        else:
            j = i - N_A_LAYERS
            (q,) = _project(x, cos_q, sin_q, [w_q[j].astype(bf16)], (True,))
            o = _attention(q, k, v)
            x = _ffn_attn_layer(x, o, w_o[j].astype(bf16), row(ln1_g[i]), row(ln1_b[i]), ffn)
        if i == N_A_LAYERS - 1:
            k, v = _project(x, cos_k, sin_k,
                            [w_kv[:, :q_dim].astype(bf16), w_kv[:, q_dim:].astype(bf16)], (True, False))
    return x
```

```python
import functools

import jax
import jax.numpy as jnp
from jax import lax
from jax.experimental import pallas as pl
from jax.experimental.pallas import tpu as pltpu

D_MODEL = 1024
DEPTH = 4
N_A_LAYERS = 2
POOL_WINDOWS = (2, 4, 8, 16)
POOL_GROUP_DIM = D_MODEL // len(POOL_WINDOWS)
POOL_HALO = 16
HEAD_DIM = 64
N_HEADS = 16
DILATED_GROUPS = ((128, 1), (512, 4), (2048, 16))
N_GROUPS = len(DILATED_GROUPS)
DILATIONS = tuple(d for _, d in DILATED_GROUPS)
SPAN = 128
ATTN_DIM = N_HEADS * HEAD_DIM
ROPE_THETA = 10000.0
D_FF = 2816
DEEPNORM_ALPHA = (2.0 * DEPTH) ** 0.25
LN_EPS = 1e-5

LANES = 128
SUBLANES = 8
MXU_DIM = 256
HEADS_PER_BLOCK = LANES // HEAD_DIM
N_HEAD_BLOCKS = ATTN_DIM // LANES
VMEM_LIMIT_BYTES = 56 * 1024 * 1024

FFN_ROWS = 512
FFN_TILES_PER_STEP = 2
FFN_COLS = MXU_DIM
FFN_PART_ROWS = 256
PROJ_ROWS = 512
MASK_VALUE = -1e30
LOG2_E = 1.4426950408889634
ATTN_HEAD_BLOCKS = 4
ATTN_LOOKAHEAD = 3

assert all(w // d == SPAN for w, d in DILATED_GROUPS)
assert D_FF % FFN_COLS == 0 and FFN_ROWS % FFN_PART_ROWS == 0


def _layer_norm(y, gain, bias):
    mu = jnp.mean(y, axis=-1, keepdims=True)
    yc = y - mu
    var = jnp.mean(yc * yc, axis=-1, keepdims=True)
    return yc * lax.rsqrt(var + LN_EPS) * gain + bias


def _resident(shape):
    return pl.BlockSpec(shape, lambda *_: (0,) * len(shape), pipeline_mode=pl.Buffered(1))


def _conv_ffn_ln(x1_parts, first_tile, wg_ref, wu_ref, cw_ref, cb_ref, wd_ref, g2_ref, b2_ref, gcarry_ref, h_ref):
    rows = FFN_PART_ROWS * len(x1_parts)
    x1b_parts = [p.astype(jnp.bfloat16) for p in x1_parts]
    for c in range(D_FF // FFN_COLS):
        cols = slice(c * FFN_COLS, (c + 1) * FFN_COLS)
        gu = [(jnp.dot(xb, wg_ref[:, cols], preferred_element_type=jnp.float32),
               jnp.dot(xb, wu_ref[:, cols], preferred_element_type=jnp.float32)) for xb in x1b_parts]
        g = jnp.concatenate([g for g, _ in gu], axis=0)
        u = jnp.concatenate([u for _, u in gu], axis=0)
        carry = jnp.where(first_tile, 0.0, gcarry_ref[:, cols])
        gcarry_ref[:, cols] = g[rows - SUBLANES:, :]
        ge = jnp.concatenate([carry, g], axis=0)
        g1 = pltpu.roll(ge, 1, 0)[SUBLANES:, :]
        g2 = pltpu.roll(ge, 2, 0)[SUBLANES:, :]
        conv = cb_ref[:, cols] + cw_ref[0:1, cols] * g2
        conv = conv + cw_ref[1:2, cols] * g1
        conv = conv + cw_ref[2:3, cols] * g
        h_ref[:, cols] = (jax.nn.gelu(conv, approximate=True) * u).astype(jnp.bfloat16)
    outs = []
    for i, x1 in enumerate(x1_parts):
        part = slice(i * FFN_PART_ROWS, (i + 1) * FFN_PART_ROWS)
        ffn = jnp.dot(h_ref[part, :], wd_ref[...], preferred_element_type=jnp.float32)
        outs.append(_layer_norm(DEEPNORM_ALPHA * x1 + ffn, g2_ref[...], b2_ref[...]))
    return jnp.concatenate(outs, axis=0)


def _pool_mix(x, halo, pos0, pw_ref, ps_ref):
    rows = x.shape[0]
    xe = jnp.concatenate([halo, x], axis=0)
    pos = pos0 + lax.broadcasted_iota(jnp.int32, (rows, 1), 0)
    mixes = []
    for gi, w in enumerate(POOL_WINDOWS):
        cols = slice(gi * POOL_GROUP_DIM, (gi + 1) * POOL_GROUP_DIM)
        s = xe[:, cols]
        k = 1
        while k < w:
            s = s + pltpu.roll(s, k, 0)
            k *= 2
        inv_count = 1.0 / jnp.minimum(pos + 1, w).astype(jnp.float32)
        pooled = s[POOL_HALO:, :] * inv_count
        diff = (pooled - x[:, cols]).astype(jnp.bfloat16)
        mixes.append(jnp.dot(diff, pw_ref[gi], preferred_element_type=jnp.float32))
    return jnp.concatenate(mixes, axis=1) * ps_ref[...]


def _ffn_pool_kernel(x_ref, pw_ref, ps_ref, g1_ref, b1_ref, wg_ref, wu_ref, cw_ref, cb_ref, wd_ref,
                     g2_ref, b2_ref, out_ref, xcarry_ref, gcarry_ref, h_ref):
    def tile_body(t, carry):
        tile = pl.program_id(1) * FFN_TILES_PER_STEP + t
        first_tile = tile == 0
        row0 = pl.multiple_of(t * FFN_ROWS, FFN_ROWS)
        halo = jnp.where(first_tile, 0.0, xcarry_ref[...])
        x1_parts = []
        for r in range(0, FFN_ROWS, FFN_PART_ROWS):
            x = x_ref[0, pl.ds(row0 + r, FFN_PART_ROWS), :]
            mix = _pool_mix(x, halo, tile * FFN_ROWS + r, pw_ref, ps_ref)
            x1_parts.append(_layer_norm(DEEPNORM_ALPHA * x + mix, g1_ref[...], b1_ref[...]))
            halo = x[FFN_PART_ROWS - POOL_HALO:, :]
        xcarry_ref[...] = halo
        out_ref[0, pl.ds(row0, FFN_ROWS), :] = _conv_ffn_ln(
            x1_parts, first_tile, wg_ref, wu_ref, cw_ref, cb_ref, wd_ref, g2_ref, b2_ref, gcarry_ref, h_ref)
        return carry

    lax.fori_loop(0, FFN_TILES_PER_STEP, tile_body, 0)


def _ffn_attn_kernel(x_ref, o_ref, wo_ref, g1_ref, b1_ref, wg_ref, wu_ref, cw_ref, cb_ref, wd_ref,
                     g2_ref, b2_ref, out_ref, gcarry_ref, h_ref):
    def tile_body(t, carry):
        first_tile = pl.program_id(1) * FFN_TILES_PER_STEP + t == 0
        row0 = pl.multiple_of(t * FFN_ROWS, FFN_ROWS)
        x1_parts = []
        for r in range(0, FFN_ROWS, FFN_PART_ROWS):
            rows = pl.ds(row0 + r, FFN_PART_ROWS)
            o = jnp.concatenate([o_ref[0, hb, rows, :] for hb in range(N_HEAD_BLOCKS)], axis=1)
            mix = jnp.dot(o, wo_ref[...], preferred_element_type=jnp.float32)
            x1_parts.append(_layer_norm(DEEPNORM_ALPHA * x_ref[0, rows, :] + mix, g1_ref[...], b1_ref[...]))
        out_ref[0, pl.ds(row0, FFN_ROWS), :] = _conv_ffn_ln(
            x1_parts, first_tile, wg_ref, wu_ref, cw_ref, cb_ref, wd_ref, g2_ref, b2_ref, gcarry_ref, h_ref)
        return carry

    lax.fori_loop(0, FFN_TILES_PER_STEP, tile_body, 0)


def _ffn_specs():
    return [_resident((D_MODEL, D_FF)), _resident((D_MODEL, D_FF)), _resident((3, D_FF)),
            _resident((1, D_FF)), _resident((D_FF, D_MODEL)), _resident((1, D_MODEL)),
            _resident((1, D_MODEL))]


def _ffn_scratch():
    return [pltpu.VMEM((SUBLANES, D_FF), jnp.float32), pltpu.VMEM((FFN_ROWS, D_FF), jnp.bfloat16)]


def _row_tile_spec():
    return pl.BlockSpec((1, FFN_TILES_PER_STEP * FFN_ROWS, D_MODEL), lambda b, t: (b, t, 0))


def _ffn_params():
    return pltpu.CompilerParams(dimension_semantics=("arbitrary", "arbitrary"),
                                vmem_limit_bytes=VMEM_LIMIT_BYTES)


def _ffn_pool_layer(x, pw, ps, g1, b1, ffn):
    batch, seq, _ = x.shape
    return pl.pallas_call(
        _ffn_pool_kernel,
        out_shape=jax.ShapeDtypeStruct(x.shape, jnp.float32),
        grid=(batch, seq // (FFN_TILES_PER_STEP * FFN_ROWS)),
        in_specs=[_row_tile_spec(),
                  _resident((len(POOL_WINDOWS), POOL_GROUP_DIM, POOL_GROUP_DIM)),
                  _resident((1, D_MODEL)), _resident((1, D_MODEL)), _resident((1, D_MODEL))]
        + _ffn_specs(),
        out_specs=_row_tile_spec(),
        scratch_shapes=[pltpu.VMEM((POOL_HALO, D_MODEL), jnp.float32)] + _ffn_scratch(),
        compiler_params=_ffn_params(),
        name="ffn_pool",
    )(x, pw, ps, g1, b1, *ffn)


def _ffn_attn_layer(x, o, wo, g1, b1, ffn):
    batch, seq, _ = x.shape
    o_spec = pl.BlockSpec((1, N_HEAD_BLOCKS, FFN_TILES_PER_STEP * FFN_ROWS, LANES), lambda b, t: (b, 0, t, 0))
    return pl.pallas_call(
        _ffn_attn_kernel,
        out_shape=jax.ShapeDtypeStruct(x.shape, jnp.float32),
        grid=(batch, seq // (FFN_TILES_PER_STEP * FFN_ROWS)),
        in_specs=[_row_tile_spec(), o_spec, _resident((ATTN_DIM, D_MODEL)),
                  _resident((1, D_MODEL)), _resident((1, D_MODEL))] + _ffn_specs(),
        out_specs=_row_tile_spec(),
        scratch_shapes=_ffn_scratch(),
        compiler_params=_ffn_params(),
        name="ffn_attn",
    )(x, o, wo, g1, b1, *ffn)


def _dilated_rows(xcol_refs, first_row, n_rows, dilation):
    seq = xcol_refs[0].shape[1]
    seg = seq // dilation
    pieces = []
    done = 0
    while done < n_rows:
        residue, m0 = divmod(first_row + done, seg)
        n = min(n_rows - done, seg - m0)
        rows = pl.ds(m0 * dilation + residue, n, stride=dilation) if dilation > 1 else pl.ds(m0, n)
        pieces.append(jnp.concatenate([ref[0, rows, :] for ref in xcol_refs], axis=1))
        done += n
    return pieces[0] if len(pieces) == 1 else jnp.concatenate(pieces, axis=0)


def _rotate_half(t, first_half):
    return jnp.where(first_half, pltpu.roll(t, LANES - HEAD_DIM // 2, 1), pltpu.roll(t, HEAD_DIM // 2, 1))


def _proj_kernel(*refs, ropes):
    n_cols, n_out = D_MODEL // LANES, len(ropes)
    xcol_refs, (cos_ref, sin_ref) = refs[:n_cols], refs[n_cols:n_cols + 2]
    w_refs, out_refs = refs[n_cols + 2:n_cols + 2 + n_out], refs[n_cols + 2 + n_out:]
    seq = cos_ref.shape[1]
    group = pl.program_id(1)
    lane = lax.broadcasted_iota(jnp.int32, (PROJ_ROWS, LANES), 1)
    first_half = (lane % HEAD_DIM) < HEAD_DIM // 2
    for gi, dilation in enumerate(DILATIONS):
        @pl.when(group == gi)
        def _(dilation=dilation):
            for c in range(seq // PROJ_ROWS):
                rows = pl.ds(c * PROJ_ROWS, PROJ_ROWS)
                xb = _dilated_rows(xcol_refs, c * PROJ_ROWS, PROJ_ROWS, dilation).astype(jnp.bfloat16)
                for w_ref, out_ref, rope in zip(w_refs, out_refs, ropes):
                    y = jnp.dot(xb, w_ref[...], preferred_element_type=jnp.float32)
                    for hb in range(N_HEAD_BLOCKS):
                        yh = y[:, hb * LANES:(hb + 1) * LANES]
                        if rope:
                            yh = yh * cos_ref[0, rows, :] + _rotate_half(yh, first_half) * sin_ref[0, rows, :]
                        out_ref[0, 0, hb, rows, :] = yh.astype(jnp.bfloat16)


def _project(x, cos, sin, weights, ropes):
    batch, seq, _ = x.shape
    out_sds = jax.ShapeDtypeStruct((batch, N_GROUPS, N_HEAD_BLOCKS, seq, LANES), jnp.bfloat16)
    table_spec = pl.BlockSpec((1, seq, LANES), lambda b, g: (g, 0, 0))
    n_cols = D_MODEL // LANES
    return pl.pallas_call(
        functools.partial(_proj_kernel, ropes=ropes),
        out_shape=[out_sds] * len(weights),
        grid=(batch, N_GROUPS),
        in_specs=[pl.BlockSpec((1, seq, LANES), lambda b, g, cb=cb: (b, 0, cb)) for cb in range(n_cols)]
        + [table_spec, table_spec]
        + [pl.BlockSpec((D_MODEL, ATTN_DIM), lambda b, g: (0, g))] * len(weights),
        out_specs=[pl.BlockSpec((1, 1, N_HEAD_BLOCKS, seq, LANES), lambda b, g: (b, g, 0, 0, 0))] * len(weights),
        compiler_params=pltpu.CompilerParams(dimension_semantics=("arbitrary", "arbitrary"),
                                             vmem_limit_bytes=VMEM_LIMIT_BYTES),
        name="proj_" + "".join("r" if r else "p" for r in ropes),
    )(*[x] * n_cols, cos, sin, *weights)


def _attn_kernel(*refs):
    assert HEADS_PER_BLOCK == 2
    q_refs, k_refs, v_refs = refs[:N_GROUPS], refs[N_GROUPS:2 * N_GROUPS], refs[2 * N_GROUPS:3 * N_GROUPS]
    o_ref, part_ref, lse_ref = refs[3 * N_GROUPS:]
    seq = o_ref.shape[1]

    def head_masks(rows, dtype):
        head = (lax.broadcasted_iota(jnp.int32, (rows, LANES), 1) // HEAD_DIM).astype(dtype)
        return [head == h for h in range(HEADS_PER_BLOCK)]

    head1_lanes = head_masks(SPAN, jnp.float32)[1]
    q_masks = head_masks(SPAN, jnp.bfloat16)
    v_masks = {rows: head_masks(rows, jnp.bfloat16) for rows in (SPAN, 2 * SPAN)}
    sum_cols = {rows: [mask.astype(jnp.bfloat16) for mask in masks] for rows, masks in v_masks.items()}
    qi = lax.broadcasted_iota(jnp.int32, (HEADS_PER_BLOCK * SPAN, 2 * SPAN), 0) % SPAN
    kj = lax.broadcasted_iota(jnp.int32, (HEADS_PER_BLOCK * SPAN, 2 * SPAN), 1)
    rel = SPAN + qi - kj
    band_bias = jnp.where((rel >= 0) & (rel <= SPAN), 0.0, MASK_VALUE).astype(jnp.float32)
    causal_bias = band_bias[:, SPAN:]

    def key_rows(gi, seg, j):
        blocks_per_seg = seq // DILATIONS[gi] // SPAN
        row0 = (seg * blocks_per_seg + j) * SPAN
        return row0, (pl.ds(row0 - SPAN, 2 * SPAN) if j > 0 else pl.ds(row0, SPAN))

    def scores(hb, gi, seg, j):
        row0, keys = key_rows(gi, seg, j)
        q = q_refs[gi][hb, pl.ds(row0, SPAN), :]
        k = k_refs[gi][hb, keys, :]
        q_heads = jnp.concatenate([jnp.where(mask, q, jnp.zeros_like(q)) for mask in q_masks], axis=0)
        s = lax.dot_general(q_heads, k, (((1,), (1,)), ((), ())), preferred_element_type=jnp.float32)
        return s + (band_bias if j > 0 else causal_bias)

    def finish(hb, gi, seg, j, s):
        row0, keys = key_rows(gi, seg, j)
        v = v_refs[gi][hb, keys, :]
        n_keys = v.shape[0]
        m = jnp.max(s, axis=-1, keepdims=True)
        p = jnp.exp2(s - m).astype(jnp.bfloat16)
        p_heads = jnp.concatenate([p[:SPAN], p[SPAN:]], axis=1)
        rhs = jnp.concatenate(
            [jnp.concatenate([jnp.where(mask, v, jnp.zeros_like(v)), ones], axis=1)
             for mask, ones in zip(v_masks[n_keys], sum_cols[n_keys])], axis=0)
        pv = jnp.dot(p_heads, rhs, preferred_element_type=jnp.float32)
        m_all = jnp.where(head1_lanes, jnp.broadcast_to(m[SPAN:], (SPAN, LANES)),
                          jnp.broadcast_to(m[:SPAN], (SPAN, LANES)))
        l = pv[:, LANES:]
        out = pv[:, :LANES] / l
        lse = m_all + jnp.log2(l)
        dilation = DILATIONS[gi]
        if dilation > 1:
            dst = pl.ds(j * SPAN * dilation + seg, SPAN, stride=dilation)
            part_ref[gi - 1, dst, :] = out
            lse_ref[gi - 1, dst, :] = lse
        else:
            rows = pl.ds(row0, SPAN)
            outs = [out] + [part_ref[g, rows, :] for g in range(N_GROUPS - 1)]
            lses = [lse] + [lse_ref[g, rows, :] for g in range(N_GROUPS - 1)]
            lse_max = functools.reduce(jnp.maximum, lses)
            es = [jnp.exp2(x - lse_max) for x in lses]
            num = sum(e * o for e, o in zip(es, outs))
            o_ref[hb, rows, :] = (num / sum(es)).astype(jnp.bfloat16)

    blocks = [(gi, seg, j)
              for gi, dilation in sorted(enumerate(DILATIONS), key=lambda e: -e[1])
              for seg in range(dilation)
              for j in range(seq // dilation // SPAN)]
    assert DILATIONS[0] == 1 and blocks[-1][0] == 0
    def head_block(hb, carry):
        pending = [scores(hb, *b) for b in blocks[:ATTN_LOOKAHEAD]]
        for i, b in enumerate(blocks):
            if i + ATTN_LOOKAHEAD < len(blocks):
                pending.append(scores(hb, *blocks[i + ATTN_LOOKAHEAD]))
            finish(hb, *b, pending.pop(0))
        return carry

    lax.fori_loop(0, o_ref.shape[0], head_block, 0)


def _attention(q, k, v):
    batch, _, _, seq, _ = q.shape
    group_specs = [pl.BlockSpec((None, None, ATTN_HEAD_BLOCKS, seq, LANES), lambda b, hb, gi=gi: (b, gi, hb, 0, 0))
                   for gi in range(N_GROUPS)]
    return pl.pallas_call(
        _attn_kernel,
        out_shape=jax.ShapeDtypeStruct((batch, N_HEAD_BLOCKS, seq, LANES), jnp.bfloat16),
        grid=(batch, N_HEAD_BLOCKS // ATTN_HEAD_BLOCKS),
        in_specs=group_specs * 3,
        out_specs=pl.BlockSpec((None, ATTN_HEAD_BLOCKS, seq, LANES), lambda b, hb: (b, hb, 0, 0)),
        scratch_shapes=[pltpu.VMEM((N_GROUPS - 1, seq, LANES), jnp.float32)] * 2,
        compiler_params=pltpu.CompilerParams(dimension_semantics=("arbitrary", "arbitrary"),
                                             vmem_limit_bytes=VMEM_LIMIT_BYTES),
        name="dilated_attn",
    )(*[q] * N_GROUPS, *[k] * N_GROUPS, *[v] * N_GROUPS)


def _rope_tables(seq, scale):
    inv_freq = ROPE_THETA ** (-jnp.arange(0, HEAD_DIM, 2, dtype=jnp.float32) / HEAD_DIM)
    lane = jnp.arange(LANES)
    freq = inv_freq[lane % (HEAD_DIM // 2)]
    sign = jnp.where((lane % HEAD_DIM) < HEAD_DIM // 2, -1.0, 1.0).astype(jnp.float32)
    row = jnp.arange(seq)
    cos, sin = [], []
    for dilation in DILATIONS:
        seg = seq // dilation
        pos = ((row % seg) * dilation + row // seg).astype(jnp.float32)
        ang = pos[:, None] * freq[None, :]
        cos.append(jnp.cos(ang) * scale)
        sin.append(jnp.sin(ang) * sign[None, :] * scale)
    return jnp.stack(cos), jnp.stack(sin)


def kernel(x, pool_w, pool_scale, w_q, w_kv, w_o, ffn_w_gate, ffn_w_up, ffn_conv_w, ffn_conv_b,
           ffn_w_down, ln1_g, ln1_b, ln2_g, ln2_b):
    batch, seq, d_model = x.shape
    assert d_model == D_MODEL and seq % (max(DILATIONS) * SPAN) == 0 and seq % (FFN_TILES_PER_STEP * FFN_ROWS) == 0
    bf16 = jnp.bfloat16
    q_dim = N_GROUPS * ATTN_DIM
    row = lambda a: a.reshape(1, -1)
    cos_q, sin_q = _rope_tables(seq, HEAD_DIM ** -0.5 * LOG2_E)
    cos_k, sin_k = _rope_tables(seq, 1.0)
    k = v = None
    for i in range(DEPTH):
        ffn = (ffn_w_gate[i].astype(bf16), ffn_w_up[i].astype(bf16), ffn_conv_w[i], row(ffn_conv_b[i]),
               ffn_w_down[i].astype(bf16), row(ln2_g[i]), row(ln2_b[i]))
        if i < N_A_LAYERS:
            x = _ffn_pool_layer(x, pool_w[i].astype(bf16), row(pool_scale[i]), row(ln1_g[i]), row(ln1_b[i]), ffn)
        else:
            j = i - N_A_LAYERS
            (q,) = _project(x, cos_q, sin_q, [w_q[j].astype(bf16)], (True,))
            o = _attention(q, k, v)
            x = _ffn_attn_layer(x, o, w_o[j].astype(bf16), row(ln1_g[i]), row(ln1_b[i]), ffn)
        if i == N_A_LAYERS - 1:
            k, v = _project(x, cos_k, sin_k,
                            [w_kv[:, :q_dim].astype(bf16), w_kv[:, q_dim:].astype(bf16)], (True, False))
    return x
```
